```python
import jax, jax.numpy as jnp
from jax import lax
import numpy as np

D_MODEL = 1024
BATCH = 2
SEQ = 8192
DEPTH = 2

N_MEM = 256
D_A = D_MODEL // 2
D_B = D_MODEL // 2
A_KERNEL = 31
B_KERNEL = 3
D_EVEN_IN = 2 * D_A + 3 * D_B
CHUNK = 128
C_GROUPS = 8
D_C = D_MODEL
C_GROUP_DIM = D_C // C_GROUPS
XA_HEADS = 4
XA_HEAD_DIM = D_MODEL // XA_HEADS
D_FF = ((8 * D_MODEL // 3 + 255) // 256) * 256
N_EVEN = (DEPTH + 1) // 2
N_ODD = DEPTH // 2
RMS_EPS = 1e-6
LN_EPS = 1e-5

kernel_name = 'hybrid_conv_sgu_memxattn_encoder'


def rms_norm(x, g):
    xf = x.astype(jnp.float32)
    y = xf * lax.rsqrt(jnp.mean(xf * xf, axis=-1, keepdims=True) + RMS_EPS)
    return (y * g.astype(jnp.float32)).astype(x.dtype)


def layer_norm(x, g, b):
    xf = x.astype(jnp.float32)
    mu = jnp.mean(xf, axis=-1, keepdims=True)
    var = jnp.mean(jnp.square(xf - mu), axis=-1, keepdims=True)
    y = (xf - mu) * lax.rsqrt(var + LN_EPS)
    return (y * g.astype(jnp.float32) + b.astype(jnp.float32)).astype(x.dtype)


def depthwise_conv(x, w, b):
    k = w.shape[0]
    pad = k // 2
    y = lax.conv_general_dilated(
        x, w[:, None, :].astype(x.dtype), window_strides=(1,),
        padding=[(pad, pad)], dimension_numbers=('NWC', 'WIO', 'NWC'),
        feature_group_count=x.shape[-1])
    return y + b


def conv_pair_mixer(n, w_in, a_conv_w, a_conv_b, a_ln_g, a_ln_b, b_conv_w, b_conv_b, w_out):
    z = n @ w_in
    a_val, a_gate, b_h, b_gb, b_gc = jnp.split(
        z, [D_A, 2 * D_A, 2 * D_A + D_B, 2 * D_A + 2 * D_B], axis=-1)
    a = a_val * jax.nn.sigmoid(a_gate)
    a = depthwise_conv(a, a_conv_w, a_conv_b)
    a = jax.nn.silu(layer_norm(a, a_ln_g, a_ln_b))
    b = b_gb * depthwise_conv(b_gc * b_h, b_conv_w, b_conv_b)
    return jnp.concatenate([a, b], axis=-1) @ w_out


def chunked_sgu_mixer(n, w_in, c_ln_g, c_ln_b, w_s, b_s, w_out):
    bsz, s, _ = n.shape
    z = jax.nn.gelu(n @ w_in)
    u, v = jnp.split(z, 2, axis=-1)
    v = layer_norm(v, c_ln_g, c_ln_b)
    v = v.reshape(bsz, s // CHUNK, CHUNK, C_GROUPS, C_GROUP_DIM)
    sv = jnp.einsum('gpq,bnqgc->bnpgc', w_s, v) + jnp.transpose(b_s)[:, :, None]
    y = u * sv.reshape(bsz, s, D_C)
    return y @ w_out


def memory_cross_attention(n, mem_n, w_q, w_k, w_v, w_o):
    bsz, s, _ = n.shape
    m = mem_n.shape[1]
    q = (n @ w_q).reshape(bsz, s, XA_HEADS, XA_HEAD_DIM)
    k = (mem_n @ w_k).reshape(bsz, m, XA_HEADS, XA_HEAD_DIM)
    v = (mem_n @ w_v).reshape(bsz, m, XA_HEADS, XA_HEAD_DIM)
    scores = jnp.einsum('bshd,bmhd->bhsm', q, k).astype(jnp.float32) * (XA_HEAD_DIM ** -0.5)
    p = jax.nn.softmax(scores, axis=-1).astype(v.dtype)
    o = jnp.einsum('bhsm,bmhd->bshd', p, v).reshape(bsz, s, D_MODEL)
    return o @ w_o


def swiglu(n, w_gate, w_up, w_down):
    return (jax.nn.silu(n @ w_gate) * (n @ w_up)) @ w_down


def setup_inputs(seed: int = 0) -> dict:
    key = jax.random.key(seed)
    ks = iter(jax.random.split(key, 40))

    def nrm(shape, scale):
        return jax.random.normal(next(ks), shape, jnp.float32) * scale

    def gain(shape):
        return 1.0 + 0.1 * jax.random.normal(next(ks), shape, jnp.float32)

    D = D_MODEL
    return {
        'x': nrm((BATCH, SEQ, D), 1.0),
        'mem': nrm((BATCH, N_MEM, D), 1.0),
        'g_mix': gain((DEPTH, D)),
        'g_xattn': gain((DEPTH, D)),
        'g_mem': gain((DEPTH, D)),
        'g_ffn': gain((DEPTH, D)),
        'g_final': gain((D,)),
        'ev_w_in': nrm((N_EVEN, D, D_EVEN_IN), D ** -0.5),
        'ev_a_conv_w': nrm((N_EVEN, A_KERNEL, D_A), A_KERNEL ** -0.5),
        'ev_a_conv_b': nrm((N_EVEN, D_A), 0.02),
        'ev_a_ln_g': gain((N_EVEN, D_A)),
        'ev_a_ln_b': nrm((N_EVEN, D_A), 0.02),
        'ev_b_conv_w': nrm((N_EVEN, B_KERNEL, D_B), B_KERNEL ** -0.5),
        'ev_b_conv_b': nrm((N_EVEN, D_B), 0.02),
        'ev_w_out': nrm((N_EVEN, D_A + D_B, D), (D_A + D_B) ** -0.5),
        'od_w_in': nrm((N_ODD, D, 2 * D_C), D ** -0.5),
        'od_c_ln_g': gain((N_ODD, D_C)),
        'od_c_ln_b': nrm((N_ODD, D_C), 0.02),
        'od_w_s': nrm((N_ODD, C_GROUPS, CHUNK, CHUNK), CHUNK ** -0.5),
        'od_b_s': gain((N_ODD, C_GROUPS, CHUNK)),
        'od_w_out': nrm((N_ODD, D_C, D), D_C ** -0.5),
        'xa_w_q': nrm((DEPTH, D, D), D ** -0.5),
        'xa_w_k': nrm((DEPTH, D, D), D ** -0.5),
        'xa_w_v': nrm((DEPTH, D, D), D ** -0.5),
        'xa_w_o': nrm((DEPTH, D, D), D ** -0.5),
        'ffn_w_gate': nrm((DEPTH, D, D_FF), D ** -0.5),
        'ffn_w_up': nrm((DEPTH, D, D_FF), D ** -0.5),
        'ffn_w_down': nrm((DEPTH, D_FF, D), D_FF ** -0.5),
    }


def reference(x, mem, g_mix, g_xattn, g_mem, g_ffn, g_final,
              ev_w_in, ev_a_conv_w, ev_a_conv_b, ev_a_ln_g, ev_a_ln_b,
              ev_b_conv_w, ev_b_conv_b, ev_w_out,
              od_w_in, od_c_ln_g, od_c_ln_b, od_w_s, od_b_s, od_w_out,
              xa_w_q, xa_w_k, xa_w_v, xa_w_o,
              ffn_w_gate, ffn_w_up, ffn_w_down):
    h = x
    for i in range(DEPTH):
        j = i // 2
        n = rms_norm(h, g_mix[i])
        if i % 2 == 0:
            h = h + conv_pair_mixer(n, ev_w_in[j], ev_a_conv_w[j], ev_a_conv_b[j],
                                    ev_a_ln_g[j], ev_a_ln_b[j], ev_b_conv_w[j],
                                    ev_b_conv_b[j], ev_w_out[j])
        else:
            h = h + chunked_sgu_mixer(n, od_w_in[j], od_c_ln_g[j], od_c_ln_b[j],
                                      od_w_s[j], od_b_s[j], od_w_out[j])
        mem_n = rms_norm(mem, g_mem[i])
        h = h + memory_cross_attention(rms_norm(h, g_xattn[i]), mem_n,
                                       xa_w_q[i], xa_w_k[i], xa_w_v[i], xa_w_o[i])
        h = h + swiglu(rms_norm(h, g_ffn[i]), ffn_w_gate[i], ffn_w_up[i], ffn_w_down[i])
    return rms_norm(h, g_final)
```

```python
import functools

import jax
import jax.numpy as jnp
from jax import lax
from jax.experimental import pallas as pl
from jax.experimental.pallas import tpu as pltpu

D_MODEL = 1024
N_MEM = 256
D_A = 512
D_B = 512
A_KERNEL = 31
B_KERNEL = 3
CHUNK = 128
C_GROUPS = 8
XA_HEADS = 4
XA_HEAD_DIM = D_MODEL // XA_HEADS
RMS_EPS = 1e-6
LN_EPS = 1e-5

HALO = 16
TM_MIX = 512
TM_FFN = 512
VMEM_LIMIT_BYTES = 56 * 1024 * 1024

_BF16 = jnp.bfloat16
_F32 = jnp.float32


def _dot(a, b):
    return jnp.dot(a, b, preferred_element_type=_F32)


def _rms(x, g):
    ms = jnp.mean(x * x, axis=-1, keepdims=True)
    return x * lax.rsqrt(ms + RMS_EPS) * g


def _layer_norm(x, g, b):
    mu = jnp.mean(x, axis=-1, keepdims=True)
    xc = x - mu
    var = jnp.mean(xc * xc, axis=-1, keepdims=True)
    return xc * lax.rsqrt(var + LN_EPS) * g + b


def _sigmoid(x):
    return 1.0 / (1.0 + jnp.exp(-x))


def _silu(x):
    return x * _sigmoid(x)


def _gelu_tanh(x):
    c = 0.7978845608028654
    return 0.5 * x * (1.0 + jnp.tanh(c * (x + 0.044715 * (x * x * x))))


def _const_spec(shape):
    zeros = (0,) * len(shape)
    return pl.BlockSpec(shape, lambda i: zeros, pipeline_mode=pl.Buffered(1))


def _params():
    return pltpu.CompilerParams(
        dimension_semantics=("arbitrary",), vmem_limit_bytes=VMEM_LIMIT_BYTES)


def _mix0_kernel(tiles_per_seq, xp_ref, xc_ref, xn_ref, g_ref, w_in_ref, a_w_ref, a_b_ref,
                 ln_g_ref, ln_b_ref, b_w_ref, b_b_ref, w_out_ref, o_ref, a_scr, c_scr):
    tm = xc_ref.shape[0]
    t = pl.program_id(0) % tiles_per_seq
    keep_prev = jnp.where(t == 0, 0.0, 1.0)
    keep_next = jnp.where(t == tiles_per_seq - 1, 0.0, 1.0)
    xc = xc_ref[...]
    xe = jnp.concatenate([xp_ref[...] * keep_prev, xc, xn_ref[...] * keep_next], axis=0)
    n = _rms(xe, g_ref[...]).astype(_BF16)
    z = _dot(n, w_in_ref[...])

    a_scr[...] = z[:, :D_A] * _sigmoid(z[:, D_A:2 * D_A])
    a_w = a_w_ref[...]
    pad_a = A_KERNEL // 2
    acc = a_b_ref[...] + a_w[0:1, :] * a_scr[pl.ds(HALO - pad_a, tm), :]
    for k in range(1, A_KERNEL):
        acc = acc + a_w[k:k + 1, :] * a_scr[pl.ds(HALO - pad_a + k, tm), :]
    a = _silu(_layer_norm(acc, ln_g_ref[...], ln_b_ref[...]))

    off_b = 2 * D_A
    c_scr[...] = z[:, off_b + 2 * D_B:off_b + 3 * D_B] * z[:, off_b:off_b + D_B]
    b_w = b_w_ref[...]
    pad_b = B_KERNEL // 2
    conv_b = b_b_ref[...] + b_w[0:1, :] * c_scr[pl.ds(HALO - pad_b, tm), :]
    for k in range(1, B_KERNEL):
        conv_b = conv_b + b_w[k:k + 1, :] * c_scr[pl.ds(HALO - pad_b + k, tm), :]
    b = z[HALO:HALO + tm, off_b + D_B:off_b + 2 * D_B] * conv_b

    y = jnp.concatenate([a, b], axis=-1).astype(_BF16)
    o_ref[...] = xc + _dot(y, w_out_ref[...])


def _mix0(h, seq, g, w_in, a_w, a_b, ln_g, ln_b, b_w, b_b, w_out):
    m, d = h.shape
    tm = TM_MIX
    assert seq % tm == 0 and tm % HALO == 0
    halo_blocks = tm // HALO
    last_halo_block = m // HALO - 1
    d_in = w_in.shape[1]
    return pl.pallas_call(
        functools.partial(_mix0_kernel, seq // tm),
        name="mix0",
        grid=(m // tm,),
        in_specs=[
            pl.BlockSpec((HALO, d), lambda i: (jnp.maximum(i * halo_blocks - 1, 0), 0)),
            pl.BlockSpec((tm, d), lambda i: (i, 0)),
            pl.BlockSpec((HALO, d), lambda i: (jnp.minimum((i + 1) * halo_blocks, last_halo_block), 0)),
            _const_spec((1, d)),
            _const_spec((d, d_in)),
            _const_spec((A_KERNEL, D_A)),
            _const_spec((1, D_A)),
            _const_spec((1, D_A)),
            _const_spec((1, D_A)),
            _const_spec((B_KERNEL, D_B)),
            _const_spec((1, D_B)),
            _const_spec((D_A + D_B, d)),
        ],
        out_specs=pl.BlockSpec((tm, d), lambda i: (i, 0)),
        out_shape=jax.ShapeDtypeStruct((m, d), _F32),
        scratch_shapes=[pltpu.VMEM((tm + 2 * HALO, D_A), _F32),
                        pltpu.VMEM((tm + 2 * HALO, D_B), _F32)],
        compiler_params=_params(),
    )(h, h, h, g, w_in, a_w, a_b, ln_g, ln_b, b_w, b_b, w_out)


def _mix1_kernel(x_ref, g_ref, w_in_ref, ln_g_ref, ln_b_ref, w_s_ref, bias_ref, w_out_ref,
                 o_ref, y_scr):
    tm, d = x_ref.shape
    n_chunks = tm // CHUNK
    gdim = d // C_GROUPS
    x = x_ref[...]
    n = _rms(x, g_ref[...]).astype(_BF16)
    z = _gelu_tanh(_dot(n, w_in_ref[...]))
    u = z[:, :d]
    v = _layer_norm(z[:, d:], ln_g_ref[...], ln_b_ref[...]).astype(_BF16)
    bias = bias_ref[...]
    for g in range(C_GROUPS):
        cols = slice(g * gdim, (g + 1) * gdim)
        v_g = jnp.concatenate([v[c * CHUNK:(c + 1) * CHUNK, cols] for c in range(n_chunks)], axis=1)
        sv_g = _dot(w_s_ref[g], v_g)
        for c in range(n_chunks):
            rows = slice(c * CHUNK, (c + 1) * CHUNK)
            sv = sv_g[:, c * gdim:(c + 1) * gdim] + bias[:, cols]
            y_scr[rows, cols] = (u[rows, cols] * sv).astype(_BF16)
    o_ref[...] = x + _dot(y_scr[...], w_out_ref[...])


def _mix1(h, g, w_in, ln_g, ln_b, w_s, bias, w_out):
    m, d = h.shape
    tm = TM_MIX
    assert tm % CHUNK == 0
    return pl.pallas_call(
        _mix1_kernel,
        name="mix1",
        grid=(m // tm,),
        in_specs=[
            pl.BlockSpec((tm, d), lambda i: (i, 0)),
            _const_spec((1, d)),
            _const_spec((d, 2 * d)),
            _const_spec((1, d)),
            _const_spec((1, d)),
            _const_spec((C_GROUPS, CHUNK, CHUNK)),
            _const_spec((CHUNK, d)),
            _const_spec((d, d)),
        ],
        out_specs=pl.BlockSpec((tm, d), lambda i: (i, 0)),
        out_shape=jax.ShapeDtypeStruct((m, d), _F32),
        scratch_shapes=[pltpu.VMEM((tm, d), _BF16)],
        compiler_params=_params(),
    )(h, g, w_in, ln_g, ln_b, w_s, bias, w_out)


def _kv_kernel(mem_ref, g_ref, w_k_ref, w_v_ref, kt_ref, v_ref):
    n = _rms(mem_ref[...], g_ref[...]).astype(_BF16)
    k = _dot(n, w_k_ref[...]) * (XA_HEAD_DIM ** -0.5)
    kt_ref[...] = k.T.astype(_BF16)
    v_ref[...] = _dot(n, w_v_ref[...]).astype(_BF16)


def _kv(mem, g, w_k, w_v):
    bsz, n_mem, d = mem.shape
    return pl.pallas_call(
        _kv_kernel,
        name="kv",
        grid=(bsz,),
        in_specs=[
            pl.BlockSpec((None, n_mem, d), lambda b: (b, 0, 0)),
            _const_spec((1, d)),
            _const_spec((d, d)),
            _const_spec((d, d)),
        ],
        out_specs=[pl.BlockSpec((None, d, n_mem), lambda b: (b, 0, 0)),
                   pl.BlockSpec((None, n_mem, d), lambda b: (b, 0, 0))],
        out_shape=[jax.ShapeDtypeStruct((bsz, d, n_mem), _BF16),
                   jax.ShapeDtypeStruct((bsz, n_mem, d), _BF16)],
        compiler_params=_params(),
    )(mem, g, w_k, w_v)


def _xattn_ffn_kernel(final_norm, x_ref, g_x_ref, w_q_ref, kt_ref, v_ref, w_o_ref,
                      g_f_ref, w_gate_ref, w_up_ref, w_down_ref, g_fin_ref, o_ref):
    x = x_ref[...]
    n = _rms(x, g_x_ref[...]).astype(_BF16)
    q = _dot(n, w_q_ref[...]).astype(_BF16)
    heads = []
    for hd in range(XA_HEADS):
        cols = slice(hd * XA_HEAD_DIM, (hd + 1) * XA_HEAD_DIM)
        s = _dot(q[:, cols], kt_ref[cols, :])
        e = jnp.exp(s - jnp.max(s, axis=-1, keepdims=True))
        p = (e / jnp.sum(e, axis=-1, keepdims=True)).astype(_BF16)
        heads.append(_dot(p, v_ref[:, cols]))
    o = jnp.concatenate(heads, axis=-1).astype(_BF16)
    h = x + _dot(o, w_o_ref[...])

    n = _rms(h, g_f_ref[...]).astype(_BF16)
    act = (_silu(_dot(n, w_gate_ref[...])) * _dot(n, w_up_ref[...])).astype(_BF16)
    h = h + _dot(act, w_down_ref[...])
    if final_norm:
        h = _rms(h, g_fin_ref[...])
    o_ref[...] = h


def _xattn_ffn(h, seq, kt, v, g_x, w_q, w_o, g_f, w_gate, w_up, w_down, g_fin, final_norm):
    m, d = h.shape
    tm = TM_FFN
    assert seq % tm == 0
    tiles_per_seq = seq // tm
    d_ff = w_gate.shape[1]
    return pl.pallas_call(
        functools.partial(_xattn_ffn_kernel, final_norm),
        name="xattn_ffn",
        grid=(m // tm,),
        in_specs=[
            pl.BlockSpec((tm, d), lambda i: (i, 0)),
            _const_spec((1, d)),
            _const_spec((d, d)),
            pl.BlockSpec((None, d, N_MEM), lambda i: (i // tiles_per_seq, 0, 0)),
            pl.BlockSpec((None, N_MEM, d), lambda i: (i // tiles_per_seq, 0, 0)),
            _const_spec((d, d)),
            _const_spec((1, d)),
            _const_spec((d, d_ff)),
            _const_spec((d, d_ff)),
            _const_spec((d_ff, d)),
            _const_spec((1, d)),
        ],
        out_specs=pl.BlockSpec((tm, d), lambda i: (i, 0)),
        out_shape=jax.ShapeDtypeStruct((m, d), _F32),
        compiler_params=_params(),
    )(h, g_x, w_q, kt, v, w_o, g_f, w_gate, w_up, w_down, g_fin)


def kernel(x, mem, g_mix, g_xattn, g_mem, g_ffn, g_final, ev_w_in, ev_a_conv_w, ev_a_conv_b, ev_a_ln_g, ev_a_ln_b, ev_b_conv_w, ev_b_conv_b, ev_w_out, od_w_in, od_c_ln_g, od_c_ln_b, od_w_s, od_b_s, od_w_out, xa_w_q, xa_w_k, xa_w_v, xa_w_o, ffn_w_gate, ffn_w_up, ffn_w_down):
    bsz, seq, d = x.shape
    depth = g_mix.shape[0]
    bf = lambda w: w.astype(_BF16)
    row = lambda p: p.reshape(1, -1)
    h = x.reshape(bsz * seq, d)
    for i in range(depth):
        j = i // 2
        if i % 2 == 0:
            h = _mix0(h, seq, row(g_mix[i]), bf(ev_w_in[j]), ev_a_conv_w[j], row(ev_a_conv_b[j]),
                      row(ev_a_ln_g[j]), row(ev_a_ln_b[j]), ev_b_conv_w[j], row(ev_b_conv_b[j]),
                      bf(ev_w_out[j]))
        else:
            bias = jnp.repeat(od_b_s[j].T, d // C_GROUPS, axis=1)
            h = _mix1(h, row(g_mix[i]), bf(od_w_in[j]), row(od_c_ln_g[j]), row(od_c_ln_b[j]),
                      bf(od_w_s[j]), bias, bf(od_w_out[j]))
        kt, v = _kv(mem, row(g_mem[i]), bf(xa_w_k[i]), bf(xa_w_v[i]))
        h = _xattn_ffn(h, seq, kt, v, row(g_xattn[i]), bf(xa_w_q[i]), bf(xa_w_o[i]),
                       row(g_ffn[i]), bf(ffn_w_gate[i]), bf(ffn_w_up[i]), bf(ffn_w_down[i]),
                       row(g_final), final_norm=(i == depth - 1))
    return h.reshape(bsz, seq, d)
```

```python
import functools

import jax
import jax.numpy as jnp
from jax import lax
from jax.experimental import pallas as pl
from jax.experimental.pallas import tpu as pltpu

D_MODEL = 1024
N_MEM = 256
D_A = 512
D_B = 512
A_KERNEL = 31
B_KERNEL = 3
CHUNK = 128
C_GROUPS = 8
XA_HEADS = 4
XA_HEAD_DIM = D_MODEL // XA_HEADS
RMS_EPS = 1e-6
LN_EPS = 1e-5

F32_SUBLANES = 8
HALO = 16
TM_MIX = 512
TM_FFN = 512
VMEM_LIMIT_BYTES = 56 * 1024 * 1024

_BF16 = jnp.bfloat16
_F32 = jnp.float32


def _dot(a, b):
    return jnp.dot(a, b, preferred_element_type=_F32)


def _rms(x, g):
    ms = jnp.mean(x * x, axis=-1, keepdims=True)
    return x * lax.rsqrt(ms + RMS_EPS) * g


def _layer_norm(x, g, b):
    mu = jnp.mean(x, axis=-1, keepdims=True)
    xc = x - mu
    var = jnp.mean(xc * xc, axis=-1, keepdims=True)
    return xc * lax.rsqrt(var + LN_EPS) * g + b


def _sigmoid(x):
    return 1.0 / (1.0 + jnp.exp(-x))


def _silu(x):
    return x * _sigmoid(x)


def _gelu_tanh(x):
    c = 0.7978845608028654
    return 0.5 * x * (1.0 + jnp.tanh(c * (x + 0.044715 * (x * x * x))))


def _const_spec(shape):
    zeros = (0,) * len(shape)
    return pl.BlockSpec(shape, lambda i: zeros, pipeline_mode=pl.Buffered(1))


def _params():
    return pltpu.CompilerParams(
        dimension_semantics=("arbitrary",), vmem_limit_bytes=VMEM_LIMIT_BYTES)


def _depthwise_conv(src_scr, shift_scr, w, bias, first, tm):
    rows = src_scr.shape[0]
    n_taps = w.shape[0]
    phases = sorted({(first + k) % F32_SUBLANES for k in range(n_taps)} - {0})
    for r in phases:
        shift_scr[r - 1, pl.ds(0, rows - F32_SUBLANES), :] = src_scr[pl.ds(r, rows - F32_SUBLANES), :]
    acc = bias
    for k in range(n_taps):
        r = (first + k) % F32_SUBLANES
        base = first + k - r
        assert base + tm <= rows - F32_SUBLANES
        if r == 0:
            tap = src_scr[pl.ds(base, tm), :]
        else:
            tap = shift_scr[r - 1, pl.ds(base, tm), :]
        acc = acc + w[k:k + 1, :] * tap
    return acc


def _mix0_kernel(tiles_per_seq, xp_ref, xc_ref, xn_ref, g_ref, w_in_ref, a_w_ref, a_b_ref,
                 ln_g_ref, ln_b_ref, b_w_ref, b_b_ref, w_out_ref, o_ref, a_scr, c_scr, shift_scr):
    tm = xc_ref.shape[0]
    t = pl.program_id(0) % tiles_per_seq
    keep_prev = jnp.where(t == 0, 0.0, 1.0)
    keep_next = jnp.where(t == tiles_per_seq - 1, 0.0, 1.0)
    xc = xc_ref[...]
    xe = jnp.concatenate([xp_ref[...] * keep_prev, xc, xn_ref[...] * keep_next], axis=0)
    n = _rms(xe, g_ref[...]).astype(_BF16)
    z = _dot(n, w_in_ref[...])

    a_scr[...] = z[:, :D_A] * _sigmoid(z[:, D_A:2 * D_A])
    conv_a = _depthwise_conv(a_scr, shift_scr, a_w_ref[...], a_b_ref[...], HALO - A_KERNEL // 2, tm)
    a = _silu(_layer_norm(conv_a, ln_g_ref[...], ln_b_ref[...]))

    off_b = 2 * D_A
    c_scr[...] = z[:, off_b + 2 * D_B:off_b + 3 * D_B] * z[:, off_b:off_b + D_B]
    conv_b = _depthwise_conv(c_scr, shift_scr, b_w_ref[...], b_b_ref[...], HALO - B_KERNEL // 2, tm)
    b = z[HALO:HALO + tm, off_b + D_B:off_b + 2 * D_B] * conv_b

    y = jnp.concatenate([a, b], axis=-1).astype(_BF16)
    o_ref[...] = xc + _dot(y, w_out_ref[...])


def _mix0(h, seq, g, w_in, a_w, a_b, ln_g, ln_b, b_w, b_b, w_out):
    m, d = h.shape
    tm = TM_MIX
    assert seq % tm == 0 and tm % HALO == 0
    assert D_A == D_B
    halo_blocks = tm // HALO
    last_halo_block = m // HALO - 1
    d_in = w_in.shape[1]
    return pl.pallas_call(
        functools.partial(_mix0_kernel, seq // tm),
        name="mix0",
        grid=(m // tm,),
        in_specs=[
            pl.BlockSpec((HALO, d), lambda i: (jnp.maximum(i * halo_blocks - 1, 0), 0)),
            pl.BlockSpec((tm, d), lambda i: (i, 0)),
            pl.BlockSpec((HALO, d), lambda i: (jnp.minimum((i + 1) * halo_blocks, last_halo_block), 0)),
            _const_spec((1, d)),
            _const_spec((d, d_in)),
            _const_spec((A_KERNEL, D_A)),
            _const_spec((1, D_A)),
            _const_spec((1, D_A)),
            _const_spec((1, D_A)),
            _const_spec((B_KERNEL, D_B)),
            _const_spec((1, D_B)),
            _const_spec((D_A + D_B, d)),
        ],
        out_specs=pl.BlockSpec((tm, d), lambda i: (i, 0)),
        out_shape=jax.ShapeDtypeStruct((m, d), _F32),
        scratch_shapes=[pltpu.VMEM((tm + 2 * HALO, D_A), _F32),
                        pltpu.VMEM((tm + 2 * HALO, D_B), _F32),
                        pltpu.VMEM((F32_SUBLANES - 1, tm + 2 * HALO, D_A), _F32)],
        compiler_params=_params(),
    )(h, h, h, g, w_in, a_w, a_b, ln_g, ln_b, b_w, b_b, w_out)


def _mix1_kernel(x_ref, g_ref, w_in_ref, ln_g_ref, ln_b_ref, w_s_ref, bias_ref, w_out_ref,
                 o_ref, y_scr):
    tm, d = x_ref.shape
    n_chunks = tm // CHUNK
    gdim = d // C_GROUPS
    x = x_ref[...]
    n = _rms(x, g_ref[...]).astype(_BF16)
    z = _gelu_tanh(_dot(n, w_in_ref[...]))
    u = z[:, :d]
    v = _layer_norm(z[:, d:], ln_g_ref[...], ln_b_ref[...]).astype(_BF16)
    bias = bias_ref[...]
    for g in range(C_GROUPS):
        cols = slice(g * gdim, (g + 1) * gdim)
        v_g = jnp.concatenate([v[c * CHUNK:(c + 1) * CHUNK, cols] for c in range(n_chunks)], axis=1)
        sv_g = _dot(w_s_ref[g], v_g)
        for c in range(n_chunks):
            rows = slice(c * CHUNK, (c + 1) * CHUNK)
            sv = sv_g[:, c * gdim:(c + 1) * gdim] + bias[:, cols]
            y_scr[rows, cols] = (u[rows, cols] * sv).astype(_BF16)
    o_ref[...] = x + _dot(y_scr[...], w_out_ref[...])


def _mix1(h, g, w_in, ln_g, ln_b, w_s, bias, w_out):
    m, d = h.shape
    tm = TM_MIX
    assert tm % CHUNK == 0
    return pl.pallas_call(
        _mix1_kernel,
        name="mix1",
        grid=(m // tm,),
        in_specs=[
            pl.BlockSpec((tm, d), lambda i: (i, 0)),
            _const_spec((1, d)),
            _const_spec((d, 2 * d)),
            _const_spec((1, d)),
            _const_spec((1, d)),
            _const_spec((C_GROUPS, CHUNK, CHUNK)),
            _const_spec((CHUNK, d)),
            _const_spec((d, d)),
        ],
        out_specs=pl.BlockSpec((tm, d), lambda i: (i, 0)),
        out_shape=jax.ShapeDtypeStruct((m, d), _F32),
        scratch_shapes=[pltpu.VMEM((tm, d), _BF16)],
        compiler_params=_params(),
    )(h, g, w_in, ln_g, ln_b, w_s, bias, w_out)


def _kv_kernel(mem_ref, g_ref, w_k_ref, w_v_ref, kt_ref, v_ref):
    n = _rms(mem_ref[...], g_ref[...]).astype(_BF16)
    k = _dot(n, w_k_ref[...]) * (XA_HEAD_DIM ** -0.5)
    kt_ref[...] = k.T.astype(_BF16)
    v_ref[...] = _dot(n, w_v_ref[...]).astype(_BF16)


def _kv(mem, g, w_k, w_v):
    bsz, n_mem, d = mem.shape
    return pl.pallas_call(
        _kv_kernel,
        name="kv",
        grid=(bsz,),
        in_specs=[
            pl.BlockSpec((None, n_mem, d), lambda b: (b, 0, 0)),
            _const_spec((1, d)),
            _const_spec((d, d)),
            _const_spec((d, d)),
        ],
        out_specs=[pl.BlockSpec((None, d, n_mem), lambda b: (b, 0, 0)),
                   pl.BlockSpec((None, n_mem, d), lambda b: (b, 0, 0))],
        out_shape=[jax.ShapeDtypeStruct((bsz, d, n_mem), _BF16),
                   jax.ShapeDtypeStruct((bsz, n_mem, d), _BF16)],
        compiler_params=_params(),
    )(mem, g, w_k, w_v)


def _xattn_ffn_kernel(final_norm, x_ref, g_x_ref, w_q_ref, kt_ref, v_ref, w_o_ref,
                      g_f_ref, w_gate_ref, w_up_ref, w_down_ref, g_fin_ref, o_ref):
    x = x_ref[...]
    n = _rms(x, g_x_ref[...]).astype(_BF16)
    q = _dot(n, w_q_ref[...]).astype(_BF16)
    heads = []
    for hd in range(XA_HEADS):
        cols = slice(hd * XA_HEAD_DIM, (hd + 1) * XA_HEAD_DIM)
        s = _dot(q[:, cols], kt_ref[cols, :])
        e = jnp.exp(s - jnp.max(s, axis=-1, keepdims=True))
        p = (e / jnp.sum(e, axis=-1, keepdims=True)).astype(_BF16)
        heads.append(_dot(p, v_ref[:, cols]))
    o = jnp.concatenate(heads, axis=-1).astype(_BF16)
    h = x + _dot(o, w_o_ref[...])

    n = _rms(h, g_f_ref[...]).astype(_BF16)
    act = (_silu(_dot(n, w_gate_ref[...])) * _dot(n, w_up_ref[...])).astype(_BF16)
    h = h + _dot(act, w_down_ref[...])
    if final_norm:
        h = _rms(h, g_fin_ref[...])
    o_ref[...] = h


def _xattn_ffn(h, seq, kt, v, g_x, w_q, w_o, g_f, w_gate, w_up, w_down, g_fin, final_norm):
    m, d = h.shape
    tm = TM_FFN
    assert seq % tm == 0
    tiles_per_seq = seq // tm
    d_ff = w_gate.shape[1]
    return pl.pallas_call(
        functools.partial(_xattn_ffn_kernel, final_norm),
        name="xattn_ffn",
        grid=(m // tm,),
        in_specs=[
            pl.BlockSpec((tm, d), lambda i: (i, 0)),
            _const_spec((1, d)),
            _const_spec((d, d)),
            pl.BlockSpec((None, d, N_MEM), lambda i: (i // tiles_per_seq, 0, 0)),
            pl.BlockSpec((None, N_MEM, d), lambda i: (i // tiles_per_seq, 0, 0)),
            _const_spec((d, d)),
            _const_spec((1, d)),
            _const_spec((d, d_ff)),
            _const_spec((d, d_ff)),
            _const_spec((d_ff, d)),
            _const_spec((1, d)),
        ],
        out_specs=pl.BlockSpec((tm, d), lambda i: (i, 0)),
        out_shape=jax.ShapeDtypeStruct((m, d), _F32),
        compiler_params=_params(),
    )(h, g_x, w_q, kt, v, w_o, g_f, w_gate, w_up, w_down, g_fin)


def kernel(x, mem, g_mix, g_xattn, g_mem, g_ffn, g_final, ev_w_in, ev_a_conv_w, ev_a_conv_b, ev_a_ln_g, ev_a_ln_b, ev_b_conv_w, ev_b_conv_b, ev_w_out, od_w_in, od_c_ln_g, od_c_ln_b, od_w_s, od_b_s, od_w_out, xa_w_q, xa_w_k, xa_w_v, xa_w_o, ffn_w_gate, ffn_w_up, ffn_w_down):
    bsz, seq, d = x.shape
    depth = g_mix.shape[0]
    bf = lambda w: w.astype(_BF16)
    row = lambda p: p.reshape(1, -1)
    h = x.reshape(bsz * seq, d)
    for i in range(depth):
        j = i // 2
        if i % 2 == 0:
            h = _mix0(h, seq, row(g_mix[i]), bf(ev_w_in[j]), ev_a_conv_w[j], row(ev_a_conv_b[j]),
                      row(ev_a_ln_g[j]), row(ev_a_ln_b[j]), ev_b_conv_w[j], row(ev_b_conv_b[j]),
                      bf(ev_w_out[j]))
        else:
            bias = jnp.repeat(od_b_s[j].T, d // C_GROUPS, axis=1)
            h = _mix1(h, row(g_mix[i]), bf(od_w_in[j]), row(od_c_ln_g[j]), row(od_c_ln_b[j]),
                      bf(od_w_s[j]), bias, bf(od_w_out[j]))
        kt, v = _kv(mem, row(g_mem[i]), bf(xa_w_k[i]), bf(xa_w_v[i]))
        h = _xattn_ffn(h, seq, kt, v, row(g_xattn[i]), bf(xa_w_q[i]), bf(xa_w_o[i]),
                       row(g_ffn[i]), bf(ffn_w_gate[i]), bf(ffn_w_up[i]), bf(ffn_w_down[i]),
                       row(g_final), final_norm=(i == depth - 1))
    return h.reshape(bsz, seq, d)
```

```python
import functools

import jax
import jax.numpy as jnp
from jax import lax
from jax.experimental import pallas as pl
from jax.experimental.pallas import tpu as pltpu

D_MODEL = 1024
N_MEM = 256
D_A = 512
D_B = 512
A_KERNEL = 31
B_KERNEL = 3
CHUNK = 128
C_GROUPS = 8
XA_HEADS = 4
XA_HEAD_DIM = D_MODEL // XA_HEADS
RMS_EPS = 1e-6
LN_EPS = 1e-5

LANES = 128
F32_SUBLANES = 8
HALO = 16
TM = 256
FF_CHUNK = 512
VMEM_LIMIT_BYTES = 56 * 1024 * 1024

_BF16 = jnp.bfloat16
_F32 = jnp.float32


def _dot(a, b):
    return jnp.dot(a, b, preferred_element_type=_F32)


def _rms(x, g):
    ms = jnp.mean(x * x, axis=-1, keepdims=True)
    return x * lax.rsqrt(ms + RMS_EPS) * g


def _layer_norm(x, g, b):
    mu = jnp.mean(x, axis=-1, keepdims=True)
    xc = x - mu
    var = jnp.mean(xc * xc, axis=-1, keepdims=True)
    return xc * lax.rsqrt(var + LN_EPS) * g + b


def _sigmoid(x):
    return 1.0 / (1.0 + jnp.exp(-x))


def _silu(x):
    return x * _sigmoid(x)


def _gelu_tanh(x):
    c = 0.7978845608028654
    return 0.5 * x * (1.0 + jnp.tanh(c * (x + 0.044715 * (x * x * x))))


def _const_spec(shape):
    zeros = (0,) * len(shape)
    return pl.BlockSpec(shape, lambda i: zeros, pipeline_mode=pl.Buffered(1))


def _params():
    return pltpu.CompilerParams(
        dimension_semantics=("arbitrary",), vmem_limit_bytes=VMEM_LIMIT_BYTES)


def _trace_alternately(*stages):
    live = list(stages)
    while live:
        for stage in list(live):
            try:
                next(stage)
            except StopIteration:
                live.remove(stage)


def _realign_phases(src_scr, shift_scr, first, n_taps):
    rows = src_scr.shape[0]
    phases = sorted({(first + k) % F32_SUBLANES for k in range(n_taps)} - {0})
    for r in phases:
        shift_scr[r - 1, pl.ds(0, rows - F32_SUBLANES), :] = src_scr[pl.ds(r, rows - F32_SUBLANES), :]


def _depthwise_conv(src_scr, shift_scr, w, bias, first, tm, cols):
    rows = src_scr.shape[0]
    acc = bias[:, cols]
    for k in range(w.shape[0]):
        r = (first + k) % F32_SUBLANES
        base = first + k - r
        assert base + tm <= rows - F32_SUBLANES
        if r == 0:
            tap = src_scr[pl.ds(base, tm), cols]
        else:
            tap = shift_scr[r - 1, pl.ds(base, tm), cols]
        acc = acc + w[k:k + 1, cols] * tap
    return acc


def _mix0_stage(first_in_seq, last_in_seq, xp_ref, xc_ref, xn_ref, g_ref, w_in_ref, a_w_ref,
                a_b_ref, ln_g_ref, ln_b_ref, b_w_ref, b_b_ref, w_out_ref, a_scr, c_scr, shift_scr,
                out_ref):
    tm = xc_ref.shape[0]
    first_a = HALO - A_KERNEL // 2
    first_b = HALO - B_KERNEL // 2
    keep_prev = jnp.where(first_in_seq, 0.0, 1.0)
    keep_next = jnp.where(last_in_seq, 0.0, 1.0)
    xe = jnp.concatenate([xp_ref[...] * keep_prev, xc_ref[...], xn_ref[...] * keep_next], axis=0)
    n = _rms(xe, g_ref[...]).astype(_BF16)
    z = _dot(n, w_in_ref[...])
    yield

    a_scr[...] = z[:, :D_A] * _sigmoid(z[:, D_A:2 * D_A])
    _realign_phases(a_scr, shift_scr, first_a, A_KERNEL)
    yield
    yield
    conv_a = []
    for j in range(D_A // LANES):
        cols = slice(j * LANES, (j + 1) * LANES)
        conv_a.append(_depthwise_conv(a_scr, shift_scr, a_w_ref[...], a_b_ref[...], first_a, tm, cols))
        yield
    a = _silu(_layer_norm(jnp.concatenate(conv_a, axis=-1), ln_g_ref[...], ln_b_ref[...]))
    yield

    off_b = 2 * D_A
    c_scr[...] = z[:, off_b + 2 * D_B:off_b + 3 * D_B] * z[:, off_b:off_b + D_B]
    _realign_phases(c_scr, shift_scr, first_b, B_KERNEL)
    conv_b = _depthwise_conv(c_scr, shift_scr, b_w_ref[...], b_b_ref[...], first_b, tm, slice(0, D_B))
    b = z[HALO:HALO + tm, off_b + D_B:off_b + 2 * D_B] * conv_b
    yield

    y = jnp.concatenate([a, b], axis=-1).astype(_BF16)
    out_ref[...] = xc_ref[...] + _dot(y, w_out_ref[...])


def _mix1_stage(x_ref, g_ref, w_in_ref, ln_g_ref, ln_b_ref, w_s_ref, bias_ref, w_out_ref, y_scr,
                out_ref):
    tm, d = x_ref.shape
    n_chunks = tm // CHUNK
    gdim = d // C_GROUPS
    n = _rms(x_ref[...], g_ref[...]).astype(_BF16)
    z = _dot(n, w_in_ref[...])
    yield
    u = _gelu_tanh(z[:, :d])
    yield
    yield
    v = _layer_norm(_gelu_tanh(z[:, d:]), ln_g_ref[...], ln_b_ref[...]).astype(_BF16)
    yield
    bias = bias_ref[...]
    for g in range(C_GROUPS):
        cols = slice(g * gdim, (g + 1) * gdim)
        v_g = jnp.concatenate([v[c * CHUNK:(c + 1) * CHUNK, cols] for c in range(n_chunks)], axis=1)
        sv_g = _dot(w_s_ref[g], v_g)
        for c in range(n_chunks):
            rows = slice(c * CHUNK, (c + 1) * CHUNK)
            sv = sv_g[:, c * gdim:(c + 1) * gdim] + bias[:, cols]
            y_scr[rows, cols] = (u[rows, cols] * sv).astype(_BF16)
    yield
    out_ref[...] = x_ref[...] + _dot(y_scr[...], w_out_ref[...])


def _xattn_ffn_stage(final_norm, x, g_x_ref, w_q_ref, kt_ref, v_ref, w_o_ref,
                     g_f_ref, w_gate_ref, w_up_ref, w_down_ref, g_fin_ref, out_ref):
    n = _rms(x, g_x_ref[...]).astype(_BF16)
    q = _dot(n, w_q_ref[...]).astype(_BF16)
    yield
    heads = []
    for hd in range(XA_HEADS):
        cols = slice(hd * XA_HEAD_DIM, (hd + 1) * XA_HEAD_DIM)
        s = _dot(q[:, cols], kt_ref[cols, :])
        e = jnp.exp(s - jnp.max(s, axis=-1, keepdims=True))
        p = (e / jnp.sum(e, axis=-1, keepdims=True)).astype(_BF16)
        heads.append(_dot(p, v_ref[:, cols]))
    yield
    o = jnp.concatenate(heads, axis=-1).astype(_BF16)
    h = x + _dot(o, w_o_ref[...])
    n = _rms(h, g_f_ref[...]).astype(_BF16)
    yield
    d_ff = w_gate_ref.shape[1]
    for c0 in range(0, d_ff, FF_CHUNK):
        cols = slice(c0, min(c0 + FF_CHUNK, d_ff))
        act = (_silu(_dot(n, w_gate_ref[:, cols])) * _dot(n, w_up_ref[:, cols])).astype(_BF16)
        h = h + _dot(act, w_down_ref[cols, :])
        yield
    if final_norm:
        h = _rms(h, g_fin_ref[...])
    out_ref[...] = h


def _run_skewed(mix_stage, xattn_ffn_stage, h_buf, o_ref):
    @pl.when(pl.program_id(0) == 0)
    def _():
        h_buf[...] = jnp.zeros(h_buf.shape, h_buf.dtype)

    x_prev = h_buf[...]
    _trace_alternately(mix_stage(out_ref=h_buf), xattn_ffn_stage(x=x_prev, out_ref=o_ref))


def _layer0_kernel(n_tiles, tiles_per_seq, xp_ref, xc_ref, xn_ref, g_ref, w_in_ref, a_w_ref,
                   a_b_ref, ln_g_ref, ln_b_ref, b_w_ref, b_b_ref, w_out_ref,
                   g_x_ref, w_q_ref, kt_ref, v_ref, w_o_ref, g_f_ref, w_gate_ref, w_up_ref,
                   w_down_ref, g_fin_ref, o_ref, h_buf, a_scr, c_scr, shift_scr):
    t = jnp.minimum(pl.program_id(0), n_tiles - 1) % tiles_per_seq
    mix = functools.partial(_mix0_stage, t == 0, t == tiles_per_seq - 1, xp_ref, xc_ref, xn_ref,
                            g_ref, w_in_ref, a_w_ref, a_b_ref, ln_g_ref, ln_b_ref, b_w_ref,
                            b_b_ref, w_out_ref, a_scr, c_scr, shift_scr)
    stage = functools.partial(_xattn_ffn_stage, False, g_x_ref=g_x_ref, w_q_ref=w_q_ref,
                              kt_ref=kt_ref, v_ref=v_ref, w_o_ref=w_o_ref, g_f_ref=g_f_ref,
                              w_gate_ref=w_gate_ref, w_up_ref=w_up_ref, w_down_ref=w_down_ref,
                              g_fin_ref=g_fin_ref)
    _run_skewed(mix, stage, h_buf, o_ref)


def _layer1_kernel(x_ref, g_ref, w_in_ref, ln_g_ref, ln_b_ref, w_s_ref, bias_ref, w_out_ref,
                   g_x_ref, w_q_ref, kt_ref, v_ref, w_o_ref, g_f_ref, w_gate_ref, w_up_ref,
                   w_down_ref, g_fin_ref, o_ref, h_buf, y_scr):
    mix = functools.partial(_mix1_stage, x_ref, g_ref, w_in_ref, ln_g_ref, ln_b_ref, w_s_ref,
                            bias_ref, w_out_ref, y_scr)
    stage = functools.partial(_xattn_ffn_stage, True, g_x_ref=g_x_ref, w_q_ref=w_q_ref,
                              kt_ref=kt_ref, v_ref=v_ref, w_o_ref=w_o_ref, g_f_ref=g_f_ref,
                              w_gate_ref=w_gate_ref, w_up_ref=w_up_ref, w_down_ref=w_down_ref,
                              g_fin_ref=g_fin_ref)
    _run_skewed(mix, stage, h_buf, o_ref)


def _xattn_ffn_specs(d, d_ff, n_tiles, tiles_per_seq):
    batch_of_prev_tile = lambda s: (jnp.maximum(s - 1, 0) // tiles_per_seq, 0, 0)
    return [
        _const_spec((1, d)),
        _const_spec((d, d)),
        pl.BlockSpec((None, d, N_MEM), batch_of_prev_tile),
        pl.BlockSpec((None, N_MEM, d), batch_of_prev_tile),
        _const_spec((d, d)),
        _const_spec((1, d)),
        _const_spec((d, d_ff)),
        _const_spec((d, d_ff)),
        _const_spec((d_ff, d)),
        _const_spec((1, d)),
    ]


def _layer0(h, seq, mix_w, xf_w):
    m, d = h.shape
    tm = TM
    assert seq % tm == 0 and tm % HALO == 0
    assert D_A == D_B
    n_tiles = m // tm
    halo_blocks = tm // HALO
    last_halo_block = m // HALO - 1
    tile = lambda s: jnp.minimum(s, n_tiles - 1)
    g, w_in, a_w, a_b, ln_g, ln_b, b_w, b_b, w_out = mix_w
    d_ff = xf_w[6].shape[1]
    return pl.pallas_call(
        functools.partial(_layer0_kernel, n_tiles, seq // tm),
        name="layer0",
        grid=(n_tiles + 1,),
        in_specs=[
            pl.BlockSpec((HALO, d), lambda s: (jnp.maximum(tile(s) * halo_blocks - 1, 0), 0)),
            pl.BlockSpec((tm, d), lambda s: (tile(s), 0)),
            pl.BlockSpec((HALO, d), lambda s: (jnp.minimum((tile(s) + 1) * halo_blocks, last_halo_block), 0)),
            _const_spec((1, d)),
            _const_spec(w_in.shape),
            _const_spec((A_KERNEL, D_A)),
            _const_spec((1, D_A)),
            _const_spec((1, D_A)),
            _const_spec((1, D_A)),
            _const_spec((B_KERNEL, D_B)),
            _const_spec((1, D_B)),
            _const_spec((D_A + D_B, d)),
        ] + _xattn_ffn_specs(d, d_ff, n_tiles, seq // tm),
        out_specs=pl.BlockSpec((tm, d), lambda s: (jnp.maximum(s - 1, 0), 0)),
        out_shape=jax.ShapeDtypeStruct((m, d), _F32),
        scratch_shapes=[pltpu.VMEM((tm, d), _F32),
                        pltpu.VMEM((tm + 2 * HALO, D_A), _F32),
                        pltpu.VMEM((tm + 2 * HALO, D_B), _F32),
                        pltpu.VMEM((F32_SUBLANES - 1, tm + 2 * HALO, D_A), _F32)],
        compiler_params=_params(),
    )(h, h, h, g, w_in, a_w, a_b, ln_g, ln_b, b_w, b_b, w_out, *xf_w)


def _layer1(h, seq, mix_w, xf_w):
    m, d = h.shape
    tm = TM
    assert seq % tm == 0 and tm % CHUNK == 0
    n_tiles = m // tm
    g, w_in, ln_g, ln_b, w_s, bias, w_out = mix_w
    d_ff = xf_w[6].shape[1]
    return pl.pallas_call(
        _layer1_kernel,
        name="layer1",
        grid=(n_tiles + 1,),
        in_specs=[
            pl.BlockSpec((tm, d), lambda s: (jnp.minimum(s, n_tiles - 1), 0)),
            _const_spec((1, d)),
            _const_spec((d, 2 * d)),
            _const_spec((1, d)),
            _const_spec((1, d)),
            _const_spec((C_GROUPS, CHUNK, CHUNK)),
            _const_spec((CHUNK, d)),
            _const_spec((d, d)),
        ] + _xattn_ffn_specs(d, d_ff, n_tiles, seq // tm),
        out_specs=pl.BlockSpec((tm, d), lambda s: (jnp.maximum(s - 1, 0), 0)),
        out_shape=jax.ShapeDtypeStruct((m, d), _F32),
        scratch_shapes=[pltpu.VMEM((tm, d), _F32),
                        pltpu.VMEM((tm, d), _BF16)],
        compiler_params=_params(),
    )(h, g, w_in, ln_g, ln_b, w_s, bias, w_out, *xf_w)


def _kv_kernel(mem_ref, g_ref, w_k_ref, w_v_ref, kt_ref, v_ref):
    n = _rms(mem_ref[...], g_ref[...]).astype(_BF16)
    k = _dot(n, w_k_ref[...]) * (XA_HEAD_DIM ** -0.5)
    kt_ref[...] = k.T.astype(_BF16)
    v_ref[...] = _dot(n, w_v_ref[...]).astype(_BF16)


def _kv(mem, g, w_k, w_v):
    bsz, n_mem, d = mem.shape
    return pl.pallas_call(
        _kv_kernel,
        name="kv",
        grid=(bsz,),
        in_specs=[
            pl.BlockSpec((None, n_mem, d), lambda b: (b, 0, 0)),
            _const_spec((1, d)),
            _const_spec((d, d)),
            _const_spec((d, d)),
        ],
        out_specs=[pl.BlockSpec((None, d, n_mem), lambda b: (b, 0, 0)),
                   pl.BlockSpec((None, n_mem, d), lambda b: (b, 0, 0))],
        out_shape=[jax.ShapeDtypeStruct((bsz, d, n_mem), _BF16),
                   jax.ShapeDtypeStruct((bsz, n_mem, d), _BF16)],
        compiler_params=_params(),
    )(mem, g, w_k, w_v)


def kernel(x, mem, g_mix, g_xattn, g_mem, g_ffn, g_final, ev_w_in, ev_a_conv_w, ev_a_conv_b, ev_a_ln_g, ev_a_ln_b, ev_b_conv_w, ev_b_conv_b, ev_w_out, od_w_in, od_c_ln_g, od_c_ln_b, od_w_s, od_b_s, od_w_out, xa_w_q, xa_w_k, xa_w_v, xa_w_o, ffn_w_gate, ffn_w_up, ffn_w_down):
    bsz, seq, d = x.shape
    depth = g_mix.shape[0]
    assert depth == 2
    bf = lambda w: w.astype(_BF16)
    row = lambda p: p.reshape(1, -1)
    h = x.reshape(bsz * seq, d)
    for i in range(depth):
        j = i // 2
        kt, v = _kv(mem, row(g_mem[i]), bf(xa_w_k[i]), bf(xa_w_v[i]))
        xf_w = (row(g_xattn[i]), bf(xa_w_q[i]), kt, v, bf(xa_w_o[i]), row(g_ffn[i]),
                bf(ffn_w_gate[i]), bf(ffn_w_up[i]), bf(ffn_w_down[i]), row(g_final))
        if i % 2 == 0:
            mix_w = (row(g_mix[i]), bf(ev_w_in[j]), ev_a_conv_w[j], row(ev_a_conv_b[j]),
                     row(ev_a_ln_g[j]), row(ev_a_ln_b[j]), ev_b_conv_w[j], row(ev_b_conv_b[j]),
                     bf(ev_w_out[j]))
            h = _layer0(h, seq, mix_w, xf_w)
        else:
            bias = jnp.repeat(od_b_s[j].T, d // C_GROUPS, axis=1)
            mix_w = (row(g_mix[i]), bf(od_w_in[j]), row(od_c_ln_g[j]), row(od_c_ln_b[j]),
                     bf(od_w_s[j]), bias, bf(od_w_out[j]))
            h = _layer1(h, seq, mix_w, xf_w)
    return h.reshape(bsz, seq, d)
```

```python
import functools

import jax
import jax.numpy as jnp
from jax import lax
from jax.experimental import pallas as pl
from jax.experimental.pallas import tpu as pltpu

D_MODEL = 1024
N_MEM = 256
D_A = 512
D_B = 512
A_KERNEL = 31
B_KERNEL = 3
CHUNK = 128
C_GROUPS = 8
XA_HEADS = 4
XA_HEAD_DIM = D_MODEL // XA_HEADS
RMS_EPS = 1e-6
LN_EPS = 1e-5

F32_SUBLANES = 8
BF16_SUBLANES = 16
HALO = 16
TM = 512
W_STEPS_MIX = 8
W_STEPS_FFN = 16
VMEM_LIMIT_BYTES = 56 * 1024 * 1024

_BF16 = jnp.bfloat16
_F32 = jnp.float32


def _dot(a, b):
    return jnp.dot(a, b, preferred_element_type=_F32)


def _rms(x, g):
    ms = jnp.mean(x * x, axis=-1, keepdims=True)
    return x * lax.rsqrt(ms + RMS_EPS) * g


def _layer_norm(x, g, b):
    mu = jnp.mean(x, axis=-1, keepdims=True)
    xc = x - mu
    var = jnp.mean(xc * xc, axis=-1, keepdims=True)
    return xc * lax.rsqrt(var + LN_EPS) * g + b


def _sigmoid(x):
    return 1.0 / (1.0 + jnp.exp(-x))


def _silu(x):
    return x * _sigmoid(x)


def _gelu_tanh(x):
    c = 0.7978845608028654
    return 0.5 * x * (1.0 + jnp.tanh(c * (x + 0.044715 * (x * x * x))))


def _const_spec(shape):
    zeros = (0,) * len(shape)
    return pl.BlockSpec(shape, lambda s: zeros, pipeline_mode=pl.Buffered(1))


def _params():
    return pltpu.CompilerParams(
        dimension_semantics=("arbitrary",), vmem_limit_bytes=VMEM_LIMIT_BYTES)


def _chunk_rows(w, n_steps):
    rows = w.shape[0] // n_steps
    assert rows * n_steps == w.shape[0] and rows % BF16_SUBLANES == 0
    return rows


def _weight_chunk_spec(w, n_steps):
    return pl.BlockSpec((_chunk_rows(w, n_steps), w.shape[1]),
                        lambda s: (jnp.minimum(s, n_steps - 1), 0))


def _weight_scratch(w):
    return pltpu.VMEM(w.shape, _BF16)


def _stage_weights(step, chunk_refs, scr_refs):
    for chunk_ref, scr_ref in zip(chunk_refs, scr_refs):
        rows = chunk_ref.shape[0]
        start = pl.multiple_of(step * rows, rows)
        scr_ref[pl.ds(start, rows), :] = chunk_ref[...].astype(_BF16)


def _tile_index(n_w_steps, n_tiles):
    return lambda s: jnp.clip(s - n_w_steps, 0, n_tiles - 1)


def _realign_phases(src_scr, shift_scr, first, n_taps):
    rows = src_scr.shape[0]
    phases = sorted({(first + k) % F32_SUBLANES for k in range(n_taps)} - {0})
    for r in phases:
        shift_scr[r - 1, pl.ds(0, rows - F32_SUBLANES), :] = src_scr[pl.ds(r, rows - F32_SUBLANES), :]


def _depthwise_conv(src_scr, shift_scr, w, bias, first, tm, cols):
    rows = src_scr.shape[0]
    acc = bias[:, cols]
    for k in range(w.shape[0]):
        r = (first + k) % F32_SUBLANES
        base = first + k - r
        assert base + tm <= rows - F32_SUBLANES
        if r == 0:
            tap = src_scr[pl.ds(base, tm), cols]
        else:
            tap = shift_scr[r - 1, pl.ds(base, tm), cols]
        acc = acc + w[k:k + 1, cols] * tap
    return acc


def _mix0_kernel(n_w_steps, tiles_per_seq, xp_ref, xc_ref, xn_ref, g_ref,
                 w_in_ref, w_out_ref, a_w_ref, a_b_ref, ln_g_ref, ln_b_ref, b_w_ref, b_b_ref,
                 o_ref, w_in_scr, w_out_scr, a_scr, c_scr, shift_scr):
    s = pl.program_id(0)

    @pl.when(s < n_w_steps)
    def _():
        _stage_weights(s, (w_in_ref, w_out_ref), (w_in_scr, w_out_scr))

    @pl.when(s >= n_w_steps)
    def _():
        tm = xc_ref.shape[0]
        first_a = HALO - A_KERNEL // 2
        first_b = HALO - B_KERNEL // 2
        t = (s - n_w_steps) % tiles_per_seq
        keep_prev = jnp.where(t == 0, 0.0, 1.0)
        keep_next = jnp.where(t == tiles_per_seq - 1, 0.0, 1.0)
        xc = xc_ref[...]
        xe = jnp.concatenate([xp_ref[...] * keep_prev, xc, xn_ref[...] * keep_next], axis=0)
        n = _rms(xe, g_ref[...]).astype(_BF16)
        z = _dot(n, w_in_scr[...])

        a_scr[...] = z[:, :D_A] * _sigmoid(z[:, D_A:2 * D_A])
        _realign_phases(a_scr, shift_scr, first_a, A_KERNEL)
        conv_a = _depthwise_conv(a_scr, shift_scr, a_w_ref[...], a_b_ref[...], first_a, tm, slice(0, D_A))
        a = _silu(_layer_norm(conv_a, ln_g_ref[...], ln_b_ref[...]))

        off_b = 2 * D_A
        c_scr[...] = z[:, off_b + 2 * D_B:off_b + 3 * D_B] * z[:, off_b:off_b + D_B]
        _realign_phases(c_scr, shift_scr, first_b, B_KERNEL)
        conv_b = _depthwise_conv(c_scr, shift_scr, b_w_ref[...], b_b_ref[...], first_b, tm, slice(0, D_B))
        b = z[HALO:HALO + tm, off_b + D_B:off_b + 2 * D_B] * conv_b

        y = jnp.concatenate([a, b], axis=-1).astype(_BF16)
        o_ref[...] = xc + _dot(y, w_out_scr[...])


def _mix0(h, seq, g, w_in, a_w, a_b, ln_g, ln_b, b_w, b_b, w_out):
    m, d = h.shape
    tm = TM
    assert seq % tm == 0 and tm % HALO == 0
    assert D_A == D_B
    n_tiles = m // tm
    n_w = W_STEPS_MIX
    halo_blocks = tm // HALO
    last_halo_block = m // HALO - 1
    tile = _tile_index(n_w, n_tiles)
    rows_ext = tm + 2 * HALO
    return pl.pallas_call(
        functools.partial(_mix0_kernel, n_w, seq // tm),
        name="mix0",
        grid=(n_w + n_tiles,),
        in_specs=[
            pl.BlockSpec((HALO, d), lambda s: (jnp.maximum(tile(s) * halo_blocks - 1, 0), 0)),
            pl.BlockSpec((tm, d), lambda s: (tile(s), 0)),
            pl.BlockSpec((HALO, d), lambda s: (jnp.minimum((tile(s) + 1) * halo_blocks, last_halo_block), 0)),
            _const_spec((1, d)),
            _weight_chunk_spec(w_in, n_w),
            _weight_chunk_spec(w_out, n_w),
            _const_spec((A_KERNEL, D_A)),
            _const_spec((1, D_A)),
            _const_spec((1, D_A)),
            _const_spec((1, D_A)),
            _const_spec((B_KERNEL, D_B)),
            _const_spec((1, D_B)),
        ],
        out_specs=pl.BlockSpec((tm, d), lambda s: (tile(s), 0)),
        out_shape=jax.ShapeDtypeStruct((m, d), _F32),
        scratch_shapes=[_weight_scratch(w_in), _weight_scratch(w_out),
                        pltpu.VMEM((rows_ext, D_A), _F32),
                        pltpu.VMEM((rows_ext, D_B), _F32),
                        pltpu.VMEM((F32_SUBLANES - 1, rows_ext, D_A), _F32)],
        compiler_params=_params(),
    )(h, h, h, g, w_in, w_out, a_w, a_b, ln_g, ln_b, b_w, b_b)


def _mix1_kernel(n_w_steps, x_ref, g_ref, w_in_ref, w_out_ref, ln_g_ref, ln_b_ref, w_s_ref,
                 bias_ref, o_ref, w_in_scr, w_out_scr, y_scr):
    s = pl.program_id(0)

    @pl.when(s < n_w_steps)
    def _():
        _stage_weights(s, (w_in_ref, w_out_ref), (w_in_scr, w_out_scr))

    @pl.when(s >= n_w_steps)
    def _():
        tm, d = x_ref.shape
        n_chunks = tm // CHUNK
        gdim = d // C_GROUPS
        x = x_ref[...]
        n = _rms(x, g_ref[...]).astype(_BF16)
        z = _gelu_tanh(_dot(n, w_in_scr[...]))
        u = z[:, :d]
        v = _layer_norm(z[:, d:], ln_g_ref[...], ln_b_ref[...]).astype(_BF16)
        bias = bias_ref[...]
        for g in range(C_GROUPS):
            cols = slice(g * gdim, (g + 1) * gdim)
            v_g = jnp.concatenate([v[c * CHUNK:(c + 1) * CHUNK, cols] for c in range(n_chunks)], axis=1)
            sv_g = _dot(w_s_ref[g].astype(_BF16), v_g)
            for c in range(n_chunks):
                rows = slice(c * CHUNK, (c + 1) * CHUNK)
                sv = sv_g[:, c * gdim:(c + 1) * gdim] + bias[:, cols]
                y_scr[rows, cols] = (u[rows, cols] * sv).astype(_BF16)
        o_ref[...] = x + _dot(y_scr[...], w_out_scr[...])


def _mix1(h, g, w_in, ln_g, ln_b, w_s, bias, w_out):
    m, d = h.shape
    tm = TM
    assert tm % CHUNK == 0
    n_tiles = m // tm
    n_w = W_STEPS_MIX
    tile = _tile_index(n_w, n_tiles)
    return pl.pallas_call(
        functools.partial(_mix1_kernel, n_w),
        name="mix1",
        grid=(n_w + n_tiles,),
        in_specs=[
            pl.BlockSpec((tm, d), lambda s: (tile(s), 0)),
            _const_spec((1, d)),
            _weight_chunk_spec(w_in, n_w),
            _weight_chunk_spec(w_out, n_w),
            _const_spec((1, d)),
            _const_spec((1, d)),
            _const_spec((C_GROUPS, CHUNK, CHUNK)),
            _const_spec((CHUNK, d)),
        ],
        out_specs=pl.BlockSpec((tm, d), lambda s: (tile(s), 0)),
        out_shape=jax.ShapeDtypeStruct((m, d), _F32),
        scratch_shapes=[_weight_scratch(w_in), _weight_scratch(w_out),
                        pltpu.VMEM((tm, d), _BF16)],
        compiler_params=_params(),
    )(h, g, w_in, w_out, ln_g, ln_b, w_s, bias)


def _kv_kernel(mem_ref, g_ref, w_k_ref, w_v_ref, kt_ref, v_ref):
    bsz, n_mem, d = v_ref.shape
    n = _rms(mem_ref[...], g_ref[...]).astype(_BF16)
    k = _dot(n, w_k_ref[...].astype(_BF16)) * (XA_HEAD_DIM ** -0.5)
    v = _dot(n, w_v_ref[...].astype(_BF16)).astype(_BF16)
    for b in range(bsz):
        rows = slice(b * n_mem, (b + 1) * n_mem)
        kt_ref[b] = k[rows, :].T.astype(_BF16)
        v_ref[b] = v[rows, :]


def _kv(mem, g, w_k, w_v):
    bsz, n_mem, d = mem.shape
    return pl.pallas_call(
        _kv_kernel,
        name="kv",
        out_shape=[jax.ShapeDtypeStruct((bsz, d, n_mem), _BF16),
                   jax.ShapeDtypeStruct((bsz, n_mem, d), _BF16)],
        compiler_params=pltpu.CompilerParams(vmem_limit_bytes=VMEM_LIMIT_BYTES),
    )(mem.reshape(bsz * n_mem, d), g, w_k, w_v)


def _xattn_ffn_kernel(n_w_steps, final_norm, x_ref, g_x_ref, kt_ref, v_ref, g_f_ref, g_fin_ref,
                      w_q_ref, w_o_ref, w_gate_ref, w_up_ref, w_down_ref, o_ref,
                      w_q_scr, w_o_scr, w_gate_scr, w_up_scr, w_down_scr):
    s = pl.program_id(0)

    @pl.when(s < n_w_steps)
    def _():
        _stage_weights(s, (w_q_ref, w_o_ref, w_gate_ref, w_up_ref, w_down_ref),
                       (w_q_scr, w_o_scr, w_gate_scr, w_up_scr, w_down_scr))

    @pl.when(s >= n_w_steps)
    def _():
        x = x_ref[...]
        n = _rms(x, g_x_ref[...]).astype(_BF16)
        q = _dot(n, w_q_scr[...]).astype(_BF16)
        heads = []
        for hd in range(XA_HEADS):
            cols = slice(hd * XA_HEAD_DIM, (hd + 1) * XA_HEAD_DIM)
            sc = _dot(q[:, cols], kt_ref[cols, :])
            e = jnp.exp(sc - jnp.max(sc, axis=-1, keepdims=True))
            p = (e / jnp.sum(e, axis=-1, keepdims=True)).astype(_BF16)
            heads.append(_dot(p, v_ref[:, cols]))
        o = jnp.concatenate(heads, axis=-1).astype(_BF16)
        h = x + _dot(o, w_o_scr[...])

        n = _rms(h, g_f_ref[...]).astype(_BF16)
        act = (_silu(_dot(n, w_gate_scr[...])) * _dot(n, w_up_scr[...])).astype(_BF16)
        h = h + _dot(act, w_down_scr[...])
        if final_norm:
            h = _rms(h, g_fin_ref[...])
        o_ref[...] = h


def _xattn_ffn(h, seq, kt, v, g_x, w_q, w_o, g_f, w_gate, w_up, w_down, g_fin, final_norm):
    m, d = h.shape
    tm = TM
    assert seq % tm == 0
    tiles_per_seq = seq // tm
    n_tiles = m // tm
    n_w = W_STEPS_FFN
    tile = _tile_index(n_w, n_tiles)
    weights = (w_q, w_o, w_gate, w_up, w_down)
    return pl.pallas_call(
        functools.partial(_xattn_ffn_kernel, n_w, final_norm),
        name="xattn_ffn",
        grid=(n_w + n_tiles,),
        in_specs=[
            pl.BlockSpec((tm, d), lambda s: (tile(s), 0)),
            _const_spec((1, d)),
            pl.BlockSpec((None, d, N_MEM), lambda s: (tile(s) // tiles_per_seq, 0, 0)),
            pl.BlockSpec((None, N_MEM, d), lambda s: (tile(s) // tiles_per_seq, 0, 0)),
            _const_spec((1, d)),
            _const_spec((1, d)),
        ] + [_weight_chunk_spec(w, n_w) for w in weights],
        out_specs=pl.BlockSpec((tm, d), lambda s: (tile(s), 0)),
        out_shape=jax.ShapeDtypeStruct((m, d), _F32),
        scratch_shapes=[_weight_scratch(w) for w in weights],
        compiler_params=_params(),
    )(h, g_x, kt, v, g_f, g_fin, *weights)


def kernel(x, mem, g_mix, g_xattn, g_mem, g_ffn, g_final, ev_w_in, ev_a_conv_w, ev_a_conv_b, ev_a_ln_g, ev_a_ln_b, ev_b_conv_w, ev_b_conv_b, ev_w_out, od_w_in, od_c_ln_g, od_c_ln_b, od_w_s, od_b_s, od_w_out, xa_w_q, xa_w_k, xa_w_v, xa_w_o, ffn_w_gate, ffn_w_up, ffn_w_down):
    bsz, seq, d = x.shape
    depth = g_mix.shape[0]
    row = lambda p: p.reshape(1, -1)
    h = x.reshape(bsz * seq, d)
    for i in range(depth):
        j = i // 2
        if i % 2 == 0:
            h = _mix0(h, seq, row(g_mix[i]), ev_w_in[j], ev_a_conv_w[j], row(ev_a_conv_b[j]),
                      row(ev_a_ln_g[j]), row(ev_a_ln_b[j]), ev_b_conv_w[j], row(ev_b_conv_b[j]),
                      ev_w_out[j])
        else:
            bias = jnp.repeat(od_b_s[j].T, d // C_GROUPS, axis=1)
            h = _mix1(h, row(g_mix[i]), od_w_in[j], row(od_c_ln_g[j]), row(od_c_ln_b[j]),
                      od_w_s[j], bias, od_w_out[j])
        kt, v = _kv(mem, row(g_mem[i]), xa_w_k[i], xa_w_v[i])
        h = _xattn_ffn(h, seq, kt, v, row(g_xattn[i]), xa_w_q[i], xa_w_o[i],
                       row(g_ffn[i]), ffn_w_gate[i], ffn_w_up[i], ffn_w_down[i],
                       row(g_final), final_norm=(i == depth - 1))
    return h.reshape(bsz, seq, d)
```

```python
import functools

import jax
import jax.numpy as jnp
from jax import lax
from jax.experimental import pallas as pl
from jax.experimental.pallas import tpu as pltpu

D_MODEL = 1024
N_MEM = 256
D_A = 512
D_B = 512
A_KERNEL = 31
B_KERNEL = 3
CHUNK = 128
C_GROUPS = 8
XA_HEADS = 4
XA_HEAD_DIM = D_MODEL // XA_HEADS
RMS_EPS = 1e-6
LN_EPS = 1e-5

F32_SUBLANES = 8
BF16_SUBLANES = 16
HALO = 16
TM = 512
W_STEPS_MIX0 = 8
VMEM_LIMIT_BYTES = 56 * 1024 * 1024

_BF16 = jnp.bfloat16
_F32 = jnp.float32


def _dot(a, b):
    return jnp.dot(a, b, preferred_element_type=_F32)


def _rms(x, g):
    ms = jnp.mean(x * x, axis=-1, keepdims=True)
    return x * lax.rsqrt(ms + RMS_EPS) * g


def _layer_norm(x, g, b):
    mu = jnp.mean(x, axis=-1, keepdims=True)
    xc = x - mu
    var = jnp.mean(xc * xc, axis=-1, keepdims=True)
    return xc * lax.rsqrt(var + LN_EPS) * g + b


def _sigmoid(x):
    return 1.0 / (1.0 + jnp.exp(-x))


def _silu(x):
    return x * _sigmoid(x)


def _gelu_tanh(x):
    c = 0.7978845608028654
    return 0.5 * x * (1.0 + jnp.tanh(c * (x + 0.044715 * (x * x * x))))


def _const_spec(shape):
    zeros = (0,) * len(shape)
    return pl.BlockSpec(shape, lambda s: zeros, pipeline_mode=pl.Buffered(1))


def _layer_spec(w, layer):
    zeros = (0,) * (w.ndim - 1)
    return pl.BlockSpec((None,) + w.shape[1:], lambda s: (layer,) + zeros,
                        pipeline_mode=pl.Buffered(1))


def _params():
    return pltpu.CompilerParams(
        dimension_semantics=("arbitrary",), vmem_limit_bytes=VMEM_LIMIT_BYTES)


def _tile_index(n_w_steps, n_tiles):
    return lambda s: jnp.clip(s - n_w_steps, 0, n_tiles - 1)


def _cast_plan(stacked_weights, layer, n_tiles, tile):
    in_specs, out_specs, out_shapes = [], [], []
    for w in stacked_weights:
        _, rows, cols = w.shape
        tiles_per_chunk = 1
        while (rows * tiles_per_chunk) % (n_tiles * BF16_SUBLANES) != 0:
            tiles_per_chunk *= 2
        chunk = rows * tiles_per_chunk // n_tiles
        in_specs.append(pl.BlockSpec(
            (None, chunk, cols), lambda s, k=tiles_per_chunk: (layer, tile(s) // k, 0)))
        out_specs.append(pl.BlockSpec(
            (chunk, cols), lambda s, k=tiles_per_chunk: (tile(s) // k, 0)))
        out_shapes.append(jax.ShapeDtypeStruct((rows, cols), _BF16))
    return in_specs, out_specs, out_shapes


def _cast_chunks(in_refs, out_refs):
    for in_ref, out_ref in zip(in_refs, out_refs):
        out_ref[...] = in_ref[...].astype(_BF16)


def _realign_phases(src_scr, shift_scr, first, n_taps):
    rows = src_scr.shape[0]
    phases = sorted({(first + k) % F32_SUBLANES for k in range(n_taps)} - {0})
    for r in phases:
        shift_scr[r - 1, pl.ds(0, rows - F32_SUBLANES), :] = src_scr[pl.ds(r, rows - F32_SUBLANES), :]


def _depthwise_conv(src_scr, shift_scr, w, bias, first, tm):
    rows = src_scr.shape[0]
    acc = bias
    for k in range(w.shape[0]):
        r = (first + k) % F32_SUBLANES
        base = first + k - r
        assert base + tm <= rows - F32_SUBLANES
        if r == 0:
            tap = src_scr[pl.ds(base, tm), :]
        else:
            tap = shift_scr[r - 1, pl.ds(base, tm), :]
        acc = acc + w[k:k + 1, :] * tap
    return acc


def _mix0_kernel(n_w_steps, tiles_per_seq, n_cast, xp_ref, xc_ref, xn_ref, g_ref,
                 w_in_ref, w_out_ref, a_w_ref, a_b_ref, ln_g_ref, ln_b_ref, b_w_ref, b_b_ref, *refs):
    cast_in, o_ref, cast_out = refs[:n_cast], refs[n_cast], refs[n_cast + 1:2 * n_cast + 1]
    w_in_scr, w_out_scr, a_scr, c_scr, shift_scr = refs[2 * n_cast + 1:]
    s = pl.program_id(0)

    @pl.when(s < n_w_steps)
    def _():
        for chunk_ref, scr_ref in ((w_in_ref, w_in_scr), (w_out_ref, w_out_scr)):
            rows = chunk_ref.shape[0]
            start = pl.multiple_of(s * rows, rows)
            scr_ref[pl.ds(start, rows), :] = chunk_ref[...].astype(_BF16)

    @pl.when(s >= n_w_steps)
    def _():
        _cast_chunks(cast_in, cast_out)
        tm = xc_ref.shape[0]
        first_a = HALO - A_KERNEL // 2
        first_b = HALO - B_KERNEL // 2
        t = (s - n_w_steps) % tiles_per_seq
        keep_prev = jnp.where(t == 0, 0.0, 1.0)
        keep_next = jnp.where(t == tiles_per_seq - 1, 0.0, 1.0)
        xc = xc_ref[...]
        xe = jnp.concatenate([xp_ref[...] * keep_prev, xc, xn_ref[...] * keep_next], axis=0)
        n = _rms(xe, g_ref[...]).astype(_BF16)
        z = _dot(n, w_in_scr[...])

        a_scr[...] = z[:, :D_A] * _sigmoid(z[:, D_A:2 * D_A])
        _realign_phases(a_scr, shift_scr, first_a, A_KERNEL)
        conv_a = _depthwise_conv(a_scr, shift_scr, a_w_ref[...], a_b_ref[...], first_a, tm)
        a = _silu(_layer_norm(conv_a, ln_g_ref[...], ln_b_ref[...]))

        off_b = 2 * D_A
        c_scr[...] = z[:, off_b + 2 * D_B:off_b + 3 * D_B] * z[:, off_b:off_b + D_B]
        _realign_phases(c_scr, shift_scr, first_b, B_KERNEL)
        conv_b = _depthwise_conv(c_scr, shift_scr, b_w_ref[...], b_b_ref[...], first_b, tm)
        b = z[HALO:HALO + tm, off_b + D_B:off_b + 2 * D_B] * conv_b

        y = jnp.concatenate([a, b], axis=-1).astype(_BF16)
        o_ref[...] = xc + _dot(y, w_out_scr[...])


def _mix0(h, seq, layer, g, w_in, a_w, a_b, ln_g, ln_b, b_w, b_b, w_out, next_weights, next_layer):
    m, d = h.shape
    tm = TM
    assert seq % tm == 0 and tm % HALO == 0
    assert D_A == D_B
    n_tiles = m // tm
    n_w = W_STEPS_MIX0
    halo_blocks = tm // HALO
    last_halo_block = m // HALO - 1
    tile = _tile_index(n_w, n_tiles)
    rows_ext = tm + 2 * HALO

    def own_chunk_spec(w):
        rows = w.shape[1] // n_w
        assert rows * n_w == w.shape[1] and rows % BF16_SUBLANES == 0
        return pl.BlockSpec((None, rows, w.shape[2]), lambda s: (layer, jnp.minimum(s, n_w - 1), 0))

    cast_in, cast_out, cast_shapes = _cast_plan(next_weights, next_layer, n_tiles, tile)
    outs = pl.pallas_call(
        functools.partial(_mix0_kernel, n_w, seq // tm, len(next_weights)),
        name="mix0",
        grid=(n_w + n_tiles,),
        in_specs=[
            pl.BlockSpec((HALO, d), lambda s: (jnp.maximum(tile(s) * halo_blocks - 1, 0), 0)),
            pl.BlockSpec((tm, d), lambda s: (tile(s), 0)),
            pl.BlockSpec((HALO, d), lambda s: (jnp.minimum((tile(s) + 1) * halo_blocks, last_halo_block), 0)),
            _const_spec((1, d)),
            own_chunk_spec(w_in),
            own_chunk_spec(w_out),
            _layer_spec(a_w, layer),
            _const_spec((1, D_A)),
            _const_spec((1, D_A)),
            _const_spec((1, D_A)),
            _layer_spec(b_w, layer),
            _const_spec((1, D_B)),
        ] + cast_in,
        out_specs=[pl.BlockSpec((tm, d), lambda s: (tile(s), 0))] + cast_out,
        out_shape=[jax.ShapeDtypeStruct((m, d), _F32)] + cast_shapes,
        scratch_shapes=[pltpu.VMEM(w_in.shape[1:], _BF16), pltpu.VMEM(w_out.shape[1:], _BF16),
                        pltpu.VMEM((rows_ext, D_A), _F32),
                        pltpu.VMEM((rows_ext, D_B), _F32),
                        pltpu.VMEM((F32_SUBLANES - 1, rows_ext, D_A), _F32)],
        compiler_params=_params(),
    )(h, h, h, g, w_in, w_out, a_w, a_b, ln_g, ln_b, b_w, b_b, *next_weights)
    return outs[0], outs[1:]


def _mix1_kernel(n_cast, x_ref, g_ref, w_in_ref, w_out_ref, ln_g_ref, ln_b_ref, w_s_ref,
                 bias_ref, *refs):
    cast_in, o_ref, cast_out = refs[:n_cast], refs[n_cast], refs[n_cast + 1:2 * n_cast + 1]
    (y_scr,) = refs[2 * n_cast + 1:]
    _cast_chunks(cast_in, cast_out)
    tm, d = x_ref.shape
    n_chunks = tm // CHUNK
    gdim = d // C_GROUPS
    x = x_ref[...]
    n = _rms(x, g_ref[...]).astype(_BF16)
    z = _gelu_tanh(_dot(n, w_in_ref[...]))
    u = z[:, :d]
    v = _layer_norm(z[:, d:], ln_g_ref[...], ln_b_ref[...]).astype(_BF16)
    bias = bias_ref[...]
    for g in range(C_GROUPS):
        cols = slice(g * gdim, (g + 1) * gdim)
        v_g = jnp.concatenate([v[c * CHUNK:(c + 1) * CHUNK, cols] for c in range(n_chunks)], axis=1)
        sv_g = _dot(w_s_ref[g].astype(_BF16), v_g)
        for c in range(n_chunks):
            rows = slice(c * CHUNK, (c + 1) * CHUNK)
            sv = sv_g[:, c * gdim:(c + 1) * gdim] + bias[:, cols]
            y_scr[rows, cols] = (u[rows, cols] * sv).astype(_BF16)
    o_ref[...] = x + _dot(y_scr[...], w_out_ref[...])


def _mix1(h, layer, g, w_in, w_out, ln_g, ln_b, w_s, bias, next_weights, next_layer):
    m, d = h.shape
    tm = TM
    assert tm % CHUNK == 0
    n_tiles = m // tm
    tile = _tile_index(0, n_tiles)
    cast_in, cast_out, cast_shapes = _cast_plan(next_weights, next_layer, n_tiles, tile)
    outs = pl.pallas_call(
        functools.partial(_mix1_kernel, len(next_weights)),
        name="mix1",
        grid=(n_tiles,),
        in_specs=[
            pl.BlockSpec((tm, d), lambda s: (s, 0)),
            _const_spec((1, d)),
            _const_spec(w_in.shape),
            _const_spec(w_out.shape),
            _const_spec((1, d)),
            _const_spec((1, d)),
            _layer_spec(w_s, layer),
            _const_spec((CHUNK, d)),
        ] + cast_in,
        out_specs=[pl.BlockSpec((tm, d), lambda s: (s, 0))] + cast_out,
        out_shape=[jax.ShapeDtypeStruct((m, d), _F32)] + cast_shapes,
        scratch_shapes=[pltpu.VMEM((tm, d), _BF16)],
        compiler_params=_params(),
    )(h, g, w_in, w_out, ln_g, ln_b, w_s, bias, *next_weights)
    return outs[0], outs[1:]


def _kv_kernel(mem_ref, g_ref, w_k_ref, w_v_ref, kt_ref, v_ref):
    bsz, n_mem, d = v_ref.shape
    n = _rms(mem_ref[...], g_ref[...]).astype(_BF16)
    k = _dot(n, w_k_ref[...].astype(_BF16)) * (XA_HEAD_DIM ** -0.5)
    v = _dot(n, w_v_ref[...].astype(_BF16)).astype(_BF16)
    for b in range(bsz):
        rows = slice(b * n_mem, (b + 1) * n_mem)
        kt_ref[b] = k[rows, :].T.astype(_BF16)
        v_ref[b] = v[rows, :]


def _kv(mem, layer, g, w_k, w_v):
    bsz, n_mem, d = mem.shape
    return pl.pallas_call(
        _kv_kernel,
        name="kv",
        grid=(1,),
        in_specs=[_const_spec((bsz * n_mem, d)), _const_spec((1, d)),
                  _layer_spec(w_k, layer), _layer_spec(w_v, layer)],
        out_specs=[pl.BlockSpec((bsz, d, n_mem), lambda s: (0, 0, 0)),
                   pl.BlockSpec((bsz, n_mem, d), lambda s: (0, 0, 0))],
        out_shape=[jax.ShapeDtypeStruct((bsz, d, n_mem), _BF16),
                   jax.ShapeDtypeStruct((bsz, n_mem, d), _BF16)],
        compiler_params=_params(),
    )(mem.reshape(bsz * n_mem, d), g, w_k, w_v)


def _xattn_ffn_kernel(final_norm, n_cast, x_ref, g_x_ref, kt_ref, v_ref, g_f_ref, g_fin_ref,
                      w_q_ref, w_o_ref, w_gate_ref, w_up_ref, w_down_ref, *refs):
    cast_in, o_ref, cast_out = refs[:n_cast], refs[n_cast], refs[n_cast + 1:]
    _cast_chunks(cast_in, cast_out)
    x = x_ref[...]
    n = _rms(x, g_x_ref[...]).astype(_BF16)
    q = _dot(n, w_q_ref[...]).astype(_BF16)
    heads = []
    for hd in range(XA_HEADS):
        cols = slice(hd * XA_HEAD_DIM, (hd + 1) * XA_HEAD_DIM)
        sc = _dot(q[:, cols], kt_ref[cols, :])
        e = jnp.exp(sc - jnp.max(sc, axis=-1, keepdims=True))
        p = (e / jnp.sum(e, axis=-1, keepdims=True)).astype(_BF16)
        heads.append(_dot(p, v_ref[:, cols]))
    o = jnp.concatenate(heads, axis=-1).astype(_BF16)
    h = x + _dot(o, w_o_ref[...])

    n = _rms(h, g_f_ref[...]).astype(_BF16)
    act = (_silu(_dot(n, w_gate_ref[...])) * _dot(n, w_up_ref[...])).astype(_BF16)
    h = h + _dot(act, w_down_ref[...])
    if final_norm:
        h = _rms(h, g_fin_ref[...])
    o_ref[...] = h


def _xattn_ffn(h, seq, kt, v, g_x, g_f, g_fin, weights, final_norm, next_weights, next_layer):
    m, d = h.shape
    tm = TM
    assert seq % tm == 0
    tiles_per_seq = seq // tm
    n_tiles = m // tm
    tile = _tile_index(0, n_tiles)
    cast_in, cast_out, cast_shapes = _cast_plan(next_weights, next_layer, n_tiles, tile)
    outs = pl.pallas_call(
        functools.partial(_xattn_ffn_kernel, final_norm, len(next_weights)),
        name="xattn_ffn",
        grid=(n_tiles,),
        in_specs=[
            pl.BlockSpec((tm, d), lambda s: (s, 0)),
            _const_spec((1, d)),
            pl.BlockSpec((None, d, N_MEM), lambda s: (s // tiles_per_seq, 0, 0)),
            pl.BlockSpec((None, N_MEM, d), lambda s: (s // tiles_per_seq, 0, 0)),
            _const_spec((1, d)),
            _const_spec((1, d)),
        ] + [_const_spec(w.shape) for w in weights] + cast_in,
        out_specs=[pl.BlockSpec((tm, d), lambda s: (s, 0))] + cast_out,
        out_shape=[jax.ShapeDtypeStruct((m, d), _F32)] + cast_shapes,
        compiler_params=_params(),
    )(h, g_x, kt, v, g_f, g_fin, *weights, *next_weights)
    return outs[0], outs[1:]


def kernel(x, mem, g_mix, g_xattn, g_mem, g_ffn, g_final, ev_w_in, ev_a_conv_w, ev_a_conv_b, ev_a_ln_g, ev_a_ln_b, ev_b_conv_w, ev_b_conv_b, ev_w_out, od_w_in, od_c_ln_g, od_c_ln_b, od_w_s, od_b_s, od_w_out, xa_w_q, xa_w_k, xa_w_v, xa_w_o, ffn_w_gate, ffn_w_up, ffn_w_down):
    bsz, seq, d = x.shape
    depth = g_mix.shape[0]
    assert depth == 2
    row = lambda p: p.reshape(1, -1)
    xf_stacked = (xa_w_q, xa_w_o, ffn_w_gate, ffn_w_up, ffn_w_down)
    h = x.reshape(bsz * seq, d)

    h, xf_w = _mix0(h, seq, 0, row(g_mix[0]), ev_w_in, ev_a_conv_w, row(ev_a_conv_b[0]),
                    row(ev_a_ln_g[0]), row(ev_a_ln_b[0]), ev_b_conv_w, row(ev_b_conv_b[0]), ev_w_out,
                    next_weights=xf_stacked, next_layer=0)
    kt, v = _kv(mem, 0, row(g_mem[0]), xa_w_k, xa_w_v)
    h, (w_in1, w_out1) = _xattn_ffn(h, seq, kt, v, row(g_xattn[0]), row(g_ffn[0]), row(g_final), xf_w,
                                    final_norm=False, next_weights=(od_w_in, od_w_out), next_layer=0)

    bias = jnp.repeat(od_b_s[0].T, d // C_GROUPS, axis=1)
    h, xf_w = _mix1(h, 0, row(g_mix[1]), w_in1, w_out1, row(od_c_ln_g[0]), row(od_c_ln_b[0]), od_w_s,
                    bias, next_weights=xf_stacked, next_layer=1)
    kt, v = _kv(mem, 1, row(g_mem[1]), xa_w_k, xa_w_v)
    h, _ = _xattn_ffn(h, seq, kt, v, row(g_xattn[1]), row(g_ffn[1]), row(g_final), xf_w,
                      final_norm=True, next_weights=(), next_layer=0)
    return h.reshape(bsz, seq, d)
```

```python
import functools

import jax
import jax.numpy as jnp
from jax import lax
from jax.experimental import pallas as pl
from jax.experimental.pallas import tpu as pltpu

D_MODEL = 1024
N_MEM = 256
D_A = 512
D_B = 512
A_KERNEL = 31
B_KERNEL = 3
CHUNK = 128
C_GROUPS = 8
XA_HEADS = 4
XA_HEAD_DIM = D_MODEL // XA_HEADS
RMS_EPS = 1e-6
LN_EPS = 1e-5

F32_SUBLANES = 8
BF16_SUBLANES = 16
HALO = 16
TM = 512
TM_FFN = 1024
FF_CHUNK = 512
IN_PROJ_CHUNK = 512
W_STEPS_MIX0 = 8
VMEM_LIMIT_BYTES = 56 * 1024 * 1024

_BF16 = jnp.bfloat16
_F32 = jnp.float32


def _dot(a, b):
    return jnp.dot(a, b, preferred_element_type=_F32)


def _rms(x, g):
    ms = jnp.mean(x * x, axis=-1, keepdims=True)
    return x * lax.rsqrt(ms + RMS_EPS) * g


def _layer_norm(x, g, b):
    mu = jnp.mean(x, axis=-1, keepdims=True)
    xc = x - mu
    var = jnp.mean(xc * xc, axis=-1, keepdims=True)
    return xc * lax.rsqrt(var + LN_EPS) * g + b


def _sigmoid(x):
    return 1.0 / (1.0 + jnp.exp(-x))


def _silu(x):
    return x * _sigmoid(x)


def _gelu_tanh(x):
    k = -2.0 * 0.7978845608028654 * 1.4426950408889634
    return x / (1.0 + jnp.exp2(x * (k + (k * 0.044715) * (x * x))))


def _const_spec(shape):
    zeros = (0,) * len(shape)
    return pl.BlockSpec(shape, lambda s: zeros, pipeline_mode=pl.Buffered(1))


def _layer_spec(w, layer):
    zeros = (0,) * (w.ndim - 1)
    return pl.BlockSpec((None,) + w.shape[1:], lambda s: (layer,) + zeros,
                        pipeline_mode=pl.Buffered(1))


def _params():
    return pltpu.CompilerParams(
        dimension_semantics=("arbitrary",), vmem_limit_bytes=VMEM_LIMIT_BYTES)


def _tile_index(n_w_steps, n_tiles):
    return lambda s: jnp.clip(s - n_w_steps, 0, n_tiles - 1)


def _cast_plan(stacked_weights, layer, n_tiles, tile):
    in_specs, out_specs, out_shapes = [], [], []
    for w in stacked_weights:
        _, rows, cols = w.shape
        tiles_per_chunk = 1
        while (rows * tiles_per_chunk) % (n_tiles * BF16_SUBLANES) != 0:
            tiles_per_chunk *= 2
        chunk = rows * tiles_per_chunk // n_tiles
        in_specs.append(pl.BlockSpec(
            (None, chunk, cols), lambda s, k=tiles_per_chunk: (layer, tile(s) // k, 0)))
        out_specs.append(pl.BlockSpec(
            (chunk, cols), lambda s, k=tiles_per_chunk: (tile(s) // k, 0)))
        out_shapes.append(jax.ShapeDtypeStruct((rows, cols), _BF16))
    return in_specs, out_specs, out_shapes


def _cast_chunks(in_refs, out_refs):
    for in_ref, out_ref in zip(in_refs, out_refs):
        out_ref[...] = in_ref[...].astype(_BF16)


def _realign_phases(src_scr, shift_scr, first, n_taps):
    rows = src_scr.shape[0]
    phases = sorted({(first + k) % F32_SUBLANES for k in range(n_taps)} - {0})
    for r in phases:
        shift_scr[r - 1, pl.ds(0, rows - F32_SUBLANES), :] = src_scr[pl.ds(r, rows - F32_SUBLANES), :]


def _depthwise_conv(src_scr, shift_scr, w, bias, first, tm):
    rows = src_scr.shape[0]
    acc = bias
    for k in range(w.shape[0]):
        r = (first + k) % F32_SUBLANES
        base = first + k - r
        assert base + tm <= rows - F32_SUBLANES
        if r == 0:
            tap = src_scr[pl.ds(base, tm), :]
        else:
            tap = shift_scr[r - 1, pl.ds(base, tm), :]
        acc = acc + w[k:k + 1, :] * tap
    return acc


def _mix0_kernel(n_w_steps, tiles_per_seq, n_cast, xp_ref, xc_ref, xn_ref, g_ref,
                 w_in_ref, w_out_ref, a_w_ref, a_b_ref, ln_g_ref, ln_b_ref, b_w_ref, b_b_ref, *refs):
    cast_in, o_ref, cast_out = refs[:n_cast], refs[n_cast], refs[n_cast + 1:2 * n_cast + 1]
    w_in_scr, w_out_scr, a_scr, c_scr, shift_scr = refs[2 * n_cast + 1:]
    s = pl.program_id(0)

    @pl.when(s < n_w_steps)
    def _():
        for chunk_ref, scr_ref in ((w_in_ref, w_in_scr), (w_out_ref, w_out_scr)):
            rows = chunk_ref.shape[0]
            start = pl.multiple_of(s * rows, rows)
            scr_ref[pl.ds(start, rows), :] = chunk_ref[...].astype(_BF16)

    @pl.when(s >= n_w_steps)
    def _():
        _cast_chunks(cast_in, cast_out)
        tm = xc_ref.shape[0]
        first_a = HALO - A_KERNEL // 2
        first_b = HALO - B_KERNEL // 2
        t = (s - n_w_steps) % tiles_per_seq
        keep_prev = jnp.where(t == 0, 0.0, 1.0)
        keep_next = jnp.where(t == tiles_per_seq - 1, 0.0, 1.0)
        xc = xc_ref[...]
        xe = jnp.concatenate([xp_ref[...] * keep_prev, xc, xn_ref[...] * keep_next], axis=0)
        n = _rms(xe, g_ref[...]).astype(_BF16)
        z = _dot(n, w_in_scr[...])

        a_scr[...] = z[:, :D_A] * _sigmoid(z[:, D_A:2 * D_A])
        _realign_phases(a_scr, shift_scr, first_a, A_KERNEL)
        conv_a = _depthwise_conv(a_scr, shift_scr, a_w_ref[...], a_b_ref[...], first_a, tm)
        a = _silu(_layer_norm(conv_a, ln_g_ref[...], ln_b_ref[...]))

        off_b = 2 * D_A
        c_scr[...] = z[:, off_b + 2 * D_B:off_b + 3 * D_B] * z[:, off_b:off_b + D_B]
        _realign_phases(c_scr, shift_scr, first_b, B_KERNEL)
        conv_b = _depthwise_conv(c_scr, shift_scr, b_w_ref[...], b_b_ref[...], first_b, tm)
        b = z[HALO:HALO + tm, off_b + D_B:off_b + 2 * D_B] * conv_b

        y = jnp.concatenate([a, b], axis=-1).astype(_BF16)
        o_ref[...] = xc + _dot(y, w_out_scr[...])


def _mix0(h, seq, layer, g, w_in, a_w, a_b, ln_g, ln_b, b_w, b_b, w_out, next_weights, next_layer):
    m, d = h.shape
    tm = TM
    assert seq % tm == 0 and tm % HALO == 0
    assert D_A == D_B
    n_tiles = m // tm
    n_w = W_STEPS_MIX0
    halo_blocks = tm // HALO
    last_halo_block = m // HALO - 1
    tile = _tile_index(n_w, n_tiles)
    rows_ext = tm + 2 * HALO

    def own_chunk_spec(w):
        rows = w.shape[1] // n_w
        assert rows * n_w == w.shape[1] and rows % BF16_SUBLANES == 0
        return pl.BlockSpec((None, rows, w.shape[2]), lambda s: (layer, jnp.minimum(s, n_w - 1), 0))

    cast_in, cast_out, cast_shapes = _cast_plan(next_weights, next_layer, n_tiles, tile)
    outs = pl.pallas_call(
        functools.partial(_mix0_kernel, n_w, seq // tm, len(next_weights)),
        name="mix0",
        grid=(n_w + n_tiles,),
        in_specs=[
            pl.BlockSpec((HALO, d), lambda s: (jnp.maximum(tile(s) * halo_blocks - 1, 0), 0)),
            pl.BlockSpec((tm, d), lambda s: (tile(s), 0)),
            pl.BlockSpec((HALO, d), lambda s: (jnp.minimum((tile(s) + 1) * halo_blocks, last_halo_block), 0)),
            _const_spec((1, d)),
            own_chunk_spec(w_in),
            own_chunk_spec(w_out),
            _layer_spec(a_w, layer),
            _const_spec((1, D_A)),
            _const_spec((1, D_A)),
            _const_spec((1, D_A)),
            _layer_spec(b_w, layer),
            _const_spec((1, D_B)),
        ] + cast_in,
        out_specs=[pl.BlockSpec((tm, d), lambda s: (tile(s), 0))] + cast_out,
        out_shape=[jax.ShapeDtypeStruct((m, d), _F32)] + cast_shapes,
        scratch_shapes=[pltpu.VMEM(w_in.shape[1:], _BF16), pltpu.VMEM(w_out.shape[1:], _BF16),
                        pltpu.VMEM((rows_ext, D_A), _F32),
                        pltpu.VMEM((rows_ext, D_B), _F32),
                        pltpu.VMEM((F32_SUBLANES - 1, rows_ext, D_A), _F32)],
        compiler_params=_params(),
    )(h, h, h, g, w_in, w_out, a_w, a_b, ln_g, ln_b, b_w, b_b, *next_weights)
    return outs[0], outs[1:]


def _mix1_kernel(n_cast, x_ref, g_ref, w_in_ref, w_out_ref, ln_g_ref, ln_b_ref, w_s_ref,
                 bias_ref, *refs):
    cast_in, o_ref, cast_out = refs[:n_cast], refs[n_cast], refs[n_cast + 1:2 * n_cast + 1]
    (y_scr,) = refs[2 * n_cast + 1:]
    _cast_chunks(cast_in, cast_out)
    tm, d = x_ref.shape
    n_chunks = tm // CHUNK
    gdim = d // C_GROUPS
    x = x_ref[...]
    n = _rms(x, g_ref[...]).astype(_BF16)
    project = lambda c0: _gelu_tanh(_dot(n, w_in_ref[:, c0:c0 + IN_PROJ_CHUNK]))
    v = jnp.concatenate([project(d + c0) for c0 in range(0, d, IN_PROJ_CHUNK)], axis=-1)
    v = _layer_norm(v, ln_g_ref[...], ln_b_ref[...]).astype(_BF16)
    u = jnp.concatenate([project(c0) for c0 in range(0, d, IN_PROJ_CHUNK)], axis=-1)
    bias = bias_ref[...]
    for g in range(C_GROUPS):
        cols = slice(g * gdim, (g + 1) * gdim)
        v_g = jnp.concatenate([v[c * CHUNK:(c + 1) * CHUNK, cols] for c in range(n_chunks)], axis=1)
        sv_g = _dot(w_s_ref[g].astype(_BF16), v_g)
        for c in range(n_chunks):
            rows = slice(c * CHUNK, (c + 1) * CHUNK)
            sv = sv_g[:, c * gdim:(c + 1) * gdim] + bias[:, cols]
            y_scr[rows, cols] = (u[rows, cols] * sv).astype(_BF16)
    o_ref[...] = x + _dot(y_scr[...], w_out_ref[...])


def _mix1(h, layer, g, w_in, w_out, ln_g, ln_b, w_s, bias, next_weights, next_layer):
    m, d = h.shape
    tm = TM
    assert tm % CHUNK == 0
    n_tiles = m // tm
    tile = _tile_index(0, n_tiles)
    cast_in, cast_out, cast_shapes = _cast_plan(next_weights, next_layer, n_tiles, tile)
    outs = pl.pallas_call(
        functools.partial(_mix1_kernel, len(next_weights)),
        name="mix1",
        grid=(n_tiles,),
        in_specs=[
            pl.BlockSpec((tm, d), lambda s: (s, 0)),
            _const_spec((1, d)),
            _const_spec(w_in.shape),
            _const_spec(w_out.shape),
            _const_spec((1, d)),
            _const_spec((1, d)),
            _layer_spec(w_s, layer),
            _const_spec((CHUNK, d)),
        ] + cast_in,
        out_specs=[pl.BlockSpec((tm, d), lambda s: (s, 0))] + cast_out,
        out_shape=[jax.ShapeDtypeStruct((m, d), _F32)] + cast_shapes,
        scratch_shapes=[pltpu.VMEM((tm, d), _BF16)],
        compiler_params=_params(),
    )(h, g, w_in, w_out, ln_g, ln_b, w_s, bias, *next_weights)
    return outs[0], outs[1:]


def _kv_kernel(mem_ref, g_ref, w_k_ref, w_v_ref, kt_ref, v_ref):
    bsz, n_mem, d = v_ref.shape
    n = _rms(mem_ref[...], g_ref[...]).astype(_BF16)
    k = _dot(n, w_k_ref[...].astype(_BF16)) * (XA_HEAD_DIM ** -0.5)
    v = _dot(n, w_v_ref[...].astype(_BF16)).astype(_BF16)
    for b in range(bsz):
        rows = slice(b * n_mem, (b + 1) * n_mem)
        kt_ref[b] = k[rows, :].T.astype(_BF16)
        v_ref[b] = v[rows, :]


def _kv(mem, layer, g, w_k, w_v):
    bsz, n_mem, d = mem.shape
    return pl.pallas_call(
        _kv_kernel,
        name="kv",
        grid=(1,),
        in_specs=[_const_spec((bsz * n_mem, d)), _const_spec((1, d)),
                  _layer_spec(w_k, layer), _layer_spec(w_v, layer)],
        out_specs=[pl.BlockSpec((bsz, d, n_mem), lambda s: (0, 0, 0)),
                   pl.BlockSpec((bsz, n_mem, d), lambda s: (0, 0, 0))],
        out_shape=[jax.ShapeDtypeStruct((bsz, d, n_mem), _BF16),
                   jax.ShapeDtypeStruct((bsz, n_mem, d), _BF16)],
        compiler_params=_params(),
    )(mem.reshape(bsz * n_mem, d), g, w_k, w_v)


def _xattn_ffn_kernel(final_norm, n_cast, x_ref, g_x_ref, kt_ref, v_ref, g_f_ref, g_fin_ref,
                      w_q_ref, w_o_ref, w_gate_ref, w_up_ref, w_down_ref, *refs):
    cast_in, o_ref, cast_out = refs[:n_cast], refs[n_cast], refs[n_cast + 1:]
    _cast_chunks(cast_in, cast_out)
    x = x_ref[...]
    n = _rms(x, g_x_ref[...]).astype(_BF16)
    q = _dot(n, w_q_ref[...]).astype(_BF16)
    heads = []
    for hd in range(XA_HEADS):
        cols = slice(hd * XA_HEAD_DIM, (hd + 1) * XA_HEAD_DIM)
        sc = _dot(q[:, cols], kt_ref[cols, :])
        e = jnp.exp(sc - jnp.max(sc, axis=-1, keepdims=True))
        p = (e / jnp.sum(e, axis=-1, keepdims=True)).astype(_BF16)
        heads.append(_dot(p, v_ref[:, cols]))
    o = jnp.concatenate(heads, axis=-1).astype(_BF16)
    h = x + _dot(o, w_o_ref[...])

    n = _rms(h, g_f_ref[...]).astype(_BF16)
    d_ff = w_gate_ref.shape[1]
    for c0 in range(0, d_ff, FF_CHUNK):
        cols = slice(c0, min(c0 + FF_CHUNK, d_ff))
        act = (_silu(_dot(n, w_gate_ref[:, cols])) * _dot(n, w_up_ref[:, cols])).astype(_BF16)
        h = h + _dot(act, w_down_ref[cols, :])
    if final_norm:
        h = _rms(h, g_fin_ref[...])
    o_ref[...] = h


def _xattn_ffn(h, seq, kt, v, g_x, g_f, g_fin, weights, final_norm, next_weights, next_layer):
    m, d = h.shape
    tm = TM_FFN
    assert seq % tm == 0
    tiles_per_seq = seq // tm
    n_tiles = m // tm
    tile = _tile_index(0, n_tiles)
    cast_in, cast_out, cast_shapes = _cast_plan(next_weights, next_layer, n_tiles, tile)
    outs = pl.pallas_call(
        functools.partial(_xattn_ffn_kernel, final_norm, len(next_weights)),
        name="xattn_ffn",
        grid=(n_tiles,),
        in_specs=[
            pl.BlockSpec((tm, d), lambda s: (s, 0)),
            _const_spec((1, d)),
            pl.BlockSpec((None, d, N_MEM), lambda s: (s // tiles_per_seq, 0, 0)),
            pl.BlockSpec((None, N_MEM, d), lambda s: (s // tiles_per_seq, 0, 0)),
            _const_spec((1, d)),
            _const_spec((1, d)),
        ] + [_const_spec(w.shape) for w in weights] + cast_in,
        out_specs=[pl.BlockSpec((tm, d), lambda s: (s, 0))] + cast_out,
        out_shape=[jax.ShapeDtypeStruct((m, d), _F32)] + cast_shapes,
        compiler_params=_params(),
    )(h, g_x, kt, v, g_f, g_fin, *weights, *next_weights)
    return outs[0], outs[1:]


def kernel(x, mem, g_mix, g_xattn, g_mem, g_ffn, g_final, ev_w_in, ev_a_conv_w, ev_a_conv_b, ev_a_ln_g, ev_a_ln_b, ev_b_conv_w, ev_b_conv_b, ev_w_out, od_w_in, od_c_ln_g, od_c_ln_b, od_w_s, od_b_s, od_w_out, xa_w_q, xa_w_k, xa_w_v, xa_w_o, ffn_w_gate, ffn_w_up, ffn_w_down):
    bsz, seq, d = x.shape
    depth = g_mix.shape[0]
    assert depth == 2
    row = lambda p: p.reshape(1, -1)
    xf_stacked = (xa_w_q, xa_w_o, ffn_w_gate, ffn_w_up, ffn_w_down)
    h = x.reshape(bsz * seq, d)

    h, xf_w = _mix0(h, seq, 0, row(g_mix[0]), ev_w_in, ev_a_conv_w, row(ev_a_conv_b[0]),
                    row(ev_a_ln_g[0]), row(ev_a_ln_b[0]), ev_b_conv_w, row(ev_b_conv_b[0]), ev_w_out,
                    next_weights=xf_stacked, next_layer=0)
    kt, v = _kv(mem, 0, row(g_mem[0]), xa_w_k, xa_w_v)
    h, (w_in1, w_out1) = _xattn_ffn(h, seq, kt, v, row(g_xattn[0]), row(g_ffn[0]), row(g_final), xf_w,
                                    final_norm=False, next_weights=(od_w_in, od_w_out), next_layer=0)

    bias = jnp.repeat(od_b_s[0].T, d // C_GROUPS, axis=1)
    h, xf_w = _mix1(h, 0, row(g_mix[1]), w_in1, w_out1, row(od_c_ln_g[0]), row(od_c_ln_b[0]), od_w_s,
                    bias, next_weights=xf_stacked, next_layer=1)
    kt, v = _kv(mem, 1, row(g_mem[1]), xa_w_k, xa_w_v)
    h, _ = _xattn_ffn(h, seq, kt, v, row(g_xattn[1]), row(g_ffn[1]), row(g_final), xf_w,
                      final_norm=True, next_weights=(), next_layer=0)
    return h.reshape(bsz, seq, d)
```

```python
import functools

import jax
import jax.numpy as jnp
from jax import lax
from jax.experimental import pallas as pl
from jax.experimental.pallas import tpu as pltpu

D_MODEL = 1024
N_MEM = 256
D_A = 512
D_B = 512
A_KERNEL = 31
B_KERNEL = 3
CHUNK = 128
C_GROUPS = 8
XA_HEADS = 4
XA_HEAD_DIM = D_MODEL // XA_HEADS
RMS_EPS = 1e-6
LN_EPS = 1e-5

F32_SUBLANES = 8
BF16_SUBLANES = 16
HALO = 16
TM_MIX0 = 512
TM_MIX1 = 1024
TM_FFN = 1024
ATTN_ROWS = 512
FF_CHUNK = 512
IN_PROJ_CHUNK = 512
W_STEPS_MIX0 = 4
VMEM_LIMIT_BYTES = 56 * 1024 * 1024

_BF16 = jnp.bfloat16
_F32 = jnp.float32


def _dot(a, b):
    return jnp.dot(a, b, preferred_element_type=_F32)


def _rms(x, g):
    ms = jnp.mean(x * x, axis=-1, keepdims=True)
    return x * lax.rsqrt(ms + RMS_EPS) * g


def _layer_norm(x, g, b):
    mu = jnp.mean(x, axis=-1, keepdims=True)
    xc = x - mu
    var = jnp.mean(xc * xc, axis=-1, keepdims=True)
    return xc * lax.rsqrt(var + LN_EPS) * g + b


def _sigmoid(x):
    return 1.0 / (1.0 + jnp.exp(-x))


def _silu(x):
    return x * _sigmoid(x)


def _gelu_tanh(x):
    k = -2.0 * 0.7978845608028654 * 1.4426950408889634
    return x / (1.0 + jnp.exp2(x * (k + (k * 0.044715) * (x * x))))


def _const_spec(shape):
    zeros = (0,) * len(shape)
    return pl.BlockSpec(shape, lambda s: zeros, pipeline_mode=pl.Buffered(1))


def _layer_spec(w, layer):
    zeros = (0,) * (w.ndim - 1)
    return pl.BlockSpec((None,) + w.shape[1:], lambda s: (layer,) + zeros,
                        pipeline_mode=pl.Buffered(1))


def _params():
    return pltpu.CompilerParams(
        dimension_semantics=("arbitrary",), vmem_limit_bytes=VMEM_LIMIT_BYTES)


def _tile_index(n_w_steps, n_tiles):
    return lambda s: jnp.clip(s - n_w_steps, 0, n_tiles - 1)


def _cast_plan(stacked_weights, layer, n_tiles, tile):
    in_specs, out_specs, out_shapes = [], [], []
    for w in stacked_weights:
        _, rows, cols = w.shape
        tiles_per_chunk = 1
        while (rows * tiles_per_chunk) % (n_tiles * BF16_SUBLANES) != 0:
            tiles_per_chunk *= 2
        chunk = rows * tiles_per_chunk // n_tiles
        in_specs.append(pl.BlockSpec(
            (None, chunk, cols), lambda s, k=tiles_per_chunk: (layer, tile(s) // k, 0)))
        out_specs.append(pl.BlockSpec(
            (chunk, cols), lambda s, k=tiles_per_chunk: (tile(s) // k, 0)))
        out_shapes.append(jax.ShapeDtypeStruct((rows, cols), _BF16))
    return in_specs, out_specs, out_shapes


def _cast_chunks(in_refs, out_refs):
    for in_ref, out_ref in zip(in_refs, out_refs):
        out_ref[...] = in_ref[...].astype(_BF16)


def _realign_phases(src_scr, shift_scr, first, n_taps):
    rows = src_scr.shape[0]
    phases = sorted({(first + k) % F32_SUBLANES for k in range(n_taps)} - {0})
    for r in phases:
        shift_scr[r - 1, pl.ds(0, rows - F32_SUBLANES), :] = src_scr[pl.ds(r, rows - F32_SUBLANES), :]


def _depthwise_conv(src_scr, shift_scr, w, bias, first, tm):
    rows = src_scr.shape[0]
    acc = bias
    for k in range(w.shape[0]):
        r = (first + k) % F32_SUBLANES
        base = first + k - r
        assert base + tm <= rows - F32_SUBLANES
        if r == 0:
            tap = src_scr[pl.ds(base, tm), :]
        else:
            tap = shift_scr[r - 1, pl.ds(base, tm), :]
        acc = acc + w[k:k + 1, :] * tap
    return acc


def _mix0_kernel(n_w_steps, tiles_per_seq, n_cast, xp_ref, xc_ref, xn_ref, g_ref,
                 w_in_ref, w_out_ref, a_w_ref, a_b_ref, ln_g_ref, ln_b_ref, b_w_ref, b_b_ref, *refs):
    cast_in, o_ref, cast_out = refs[:n_cast], refs[n_cast], refs[n_cast + 1:2 * n_cast + 1]
    w_in_scr, w_out_scr, a_scr, c_scr, shift_scr = refs[2 * n_cast + 1:]
    s = pl.program_id(0)

    @pl.when(s < n_w_steps)
    def _():
        for chunk_ref, scr_ref in ((w_in_ref, w_in_scr), (w_out_ref, w_out_scr)):
            rows = chunk_ref.shape[0]
            start = pl.multiple_of(s * rows, rows)
            scr_ref[pl.ds(start, rows), :] = chunk_ref[...].astype(_BF16)

    @pl.when(s >= n_w_steps)
    def _():
        _cast_chunks(cast_in, cast_out)
        tm = xc_ref.shape[0]
        first_a = HALO - A_KERNEL // 2
        first_b = HALO - B_KERNEL // 2
        t = (s - n_w_steps) % tiles_per_seq
        keep_prev = jnp.where(t == 0, 0.0, 1.0)
        keep_next = jnp.where(t == tiles_per_seq - 1, 0.0, 1.0)
        xc = xc_ref[...]
        xe = jnp.concatenate([xp_ref[...] * keep_prev, xc, xn_ref[...] * keep_next], axis=0)
        n = _rms(xe, g_ref[...]).astype(_BF16)
        z = _dot(n, w_in_scr[...])

        a_scr[...] = z[:, :D_A] * _sigmoid(z[:, D_A:2 * D_A])
        _realign_phases(a_scr, shift_scr, first_a, A_KERNEL)
        conv_a = _depthwise_conv(a_scr, shift_scr, a_w_ref[...], a_b_ref[...], first_a, tm)
        a = _silu(_layer_norm(conv_a, ln_g_ref[...], ln_b_ref[...]))

        off_b = 2 * D_A
        c_scr[...] = z[:, off_b + 2 * D_B:off_b + 3 * D_B] * z[:, off_b:off_b + D_B]
        _realign_phases(c_scr, shift_scr, first_b, B_KERNEL)
        conv_b = _depthwise_conv(c_scr, shift_scr, b_w_ref[...], b_b_ref[...], first_b, tm)
        b = z[HALO:HALO + tm, off_b + D_B:off_b + 2 * D_B] * conv_b

        y = jnp.concatenate([a, b], axis=-1).astype(_BF16)
        o_ref[...] = xc + _dot(y, w_out_scr[...])


def _mix0(h, seq, layer, g, w_in, a_w, a_b, ln_g, ln_b, b_w, b_b, w_out, next_weights, next_layer):
    m, d = h.shape
    tm = TM_MIX0
    assert seq % tm == 0 and tm % HALO == 0
    assert D_A == D_B
    n_tiles = m // tm
    n_w = W_STEPS_MIX0
    halo_blocks = tm // HALO
    last_halo_block = m // HALO - 1
    tile = _tile_index(n_w, n_tiles)
    rows_ext = tm + 2 * HALO

    def own_chunk_spec(w):
        rows = w.shape[1] // n_w
        assert rows * n_w == w.shape[1] and rows % BF16_SUBLANES == 0
        return pl.BlockSpec((None, rows, w.shape[2]), lambda s: (layer, jnp.minimum(s, n_w - 1), 0))

    cast_in, cast_out, cast_shapes = _cast_plan(next_weights, next_layer, n_tiles, tile)
    outs = pl.pallas_call(
        functools.partial(_mix0_kernel, n_w, seq // tm, len(next_weights)),
        name="mix0",
        grid=(n_w + n_tiles,),
        in_specs=[
            pl.BlockSpec((HALO, d), lambda s: (jnp.maximum(tile(s) * halo_blocks - 1, 0), 0)),
            pl.BlockSpec((tm, d), lambda s: (tile(s), 0)),
            pl.BlockSpec((HALO, d), lambda s: (jnp.minimum((tile(s) + 1) * halo_blocks, last_halo_block), 0)),
            _const_spec((1, d)),
            own_chunk_spec(w_in),
            own_chunk_spec(w_out),
            _layer_spec(a_w, layer),
            _const_spec((1, D_A)),
            _const_spec((1, D_A)),
            _const_spec((1, D_A)),
            _layer_spec(b_w, layer),
            _const_spec((1, D_B)),
        ] + cast_in,
        out_specs=[pl.BlockSpec((tm, d), lambda s: (tile(s), 0))] + cast_out,
        out_shape=[jax.ShapeDtypeStruct((m, d), _F32)] + cast_shapes,
        scratch_shapes=[pltpu.VMEM(w_in.shape[1:], _BF16), pltpu.VMEM(w_out.shape[1:], _BF16),
                        pltpu.VMEM((rows_ext, D_A), _F32),
                        pltpu.VMEM((rows_ext, D_B), _F32),
                        pltpu.VMEM((F32_SUBLANES - 1, rows_ext, D_A), _F32)],
        compiler_params=_params(),
    )(h, h, h, g, w_in, w_out, a_w, a_b, ln_g, ln_b, b_w, b_b, *next_weights)
    return outs[0], outs[1:]


def _mix1_kernel(n_cast, x_ref, g_ref, w_in_ref, w_out_ref, ln_g_ref, ln_b_ref, w_s_ref,
                 bias_ref, *refs):
    cast_in, o_ref, cast_out = refs[:n_cast], refs[n_cast], refs[n_cast + 1:2 * n_cast + 1]
    (y_scr,) = refs[2 * n_cast + 1:]
    _cast_chunks(cast_in, cast_out)
    tm, d = x_ref.shape
    n_chunks = tm // CHUNK
    gdim = d // C_GROUPS
    x = x_ref[...]
    n = _rms(x, g_ref[...]).astype(_BF16)
    project = lambda c0: _gelu_tanh(_dot(n, w_in_ref[:, c0:c0 + IN_PROJ_CHUNK]))
    v = jnp.concatenate([project(d + c0) for c0 in range(0, d, IN_PROJ_CHUNK)], axis=-1)
    v = _layer_norm(v, ln_g_ref[...], ln_b_ref[...]).astype(_BF16)
    u = jnp.concatenate([project(c0) for c0 in range(0, d, IN_PROJ_CHUNK)], axis=-1)
    bias = bias_ref[...]
    for g in range(C_GROUPS):
        cols = slice(g * gdim, (g + 1) * gdim)
        v_g = jnp.concatenate([v[c * CHUNK:(c + 1) * CHUNK, cols] for c in range(n_chunks)], axis=1)
        sv_g = _dot(w_s_ref[g].astype(_BF16), v_g)
        for c in range(n_chunks):
            rows = slice(c * CHUNK, (c + 1) * CHUNK)
            sv = sv_g[:, c * gdim:(c + 1) * gdim] + bias[:, cols]
            y_scr[rows, cols] = (u[rows, cols] * sv).astype(_BF16)
    o_ref[...] = x + _dot(y_scr[...], w_out_ref[...])


def _mix1(h, layer, g, w_in, w_out, ln_g, ln_b, w_s, bias, next_weights, next_layer):
    m, d = h.shape
    tm = TM_MIX1
    assert tm % CHUNK == 0
    n_tiles = m // tm
    tile = _tile_index(0, n_tiles)
    cast_in, cast_out, cast_shapes = _cast_plan(next_weights, next_layer, n_tiles, tile)
    outs = pl.pallas_call(
        functools.partial(_mix1_kernel, len(next_weights)),
        name="mix1",
        grid=(n_tiles,),
        in_specs=[
            pl.BlockSpec((tm, d), lambda s: (s, 0)),
            _const_spec((1, d)),
            _const_spec(w_in.shape),
            _const_spec(w_out.shape),
            _const_spec((1, d)),
            _const_spec((1, d)),
            _layer_spec(w_s, layer),
            _const_spec((CHUNK, d)),
        ] + cast_in,
        out_specs=[pl.BlockSpec((tm, d), lambda s: (s, 0))] + cast_out,
        out_shape=[jax.ShapeDtypeStruct((m, d), _F32)] + cast_shapes,
        scratch_shapes=[pltpu.VMEM((tm, d), _BF16)],
        compiler_params=_params(),
    )(h, g, w_in, w_out, ln_g, ln_b, w_s, bias, *next_weights)
    return outs[0], outs[1:]


def _kv_kernel(mem_ref, g_ref, w_k_ref, w_v_ref, kt_ref, v_ref):
    bsz, n_mem, d = v_ref.shape
    n = _rms(mem_ref[...], g_ref[...]).astype(_BF16)
    k = _dot(n, w_k_ref[...].astype(_BF16)) * (XA_HEAD_DIM ** -0.5)
    v = _dot(n, w_v_ref[...].astype(_BF16)).astype(_BF16)
    for b in range(bsz):
        rows = slice(b * n_mem, (b + 1) * n_mem)
        kt_ref[b] = k[rows, :].T.astype(_BF16)
        v_ref[b] = v[rows, :]


def _kv(mem, layer, g, w_k, w_v):
    bsz, n_mem, d = mem.shape
    return pl.pallas_call(
        _kv_kernel,
        name="kv",
        grid=(1,),
        in_specs=[_const_spec((bsz * n_mem, d)), _const_spec((1, d)),
                  _layer_spec(w_k, layer), _layer_spec(w_v, layer)],
        out_specs=[pl.BlockSpec((bsz, d, n_mem), lambda s: (0, 0, 0)),
                   pl.BlockSpec((bsz, n_mem, d), lambda s: (0, 0, 0))],
        out_shape=[jax.ShapeDtypeStruct((bsz, d, n_mem), _BF16),
                   jax.ShapeDtypeStruct((bsz, n_mem, d), _BF16)],
        compiler_params=_params(),
    )(mem.reshape(bsz * n_mem, d), g, w_k, w_v)


def _xattn_ffn_kernel(final_norm, n_cast, x_ref, g_x_ref, kt_ref, v_ref, g_f_ref, g_fin_ref,
                      w_q_ref, w_o_ref, w_gate_ref, w_up_ref, w_down_ref, *refs):
    cast_in, o_ref, cast_out = refs[:n_cast], refs[n_cast], refs[n_cast + 1:]
    _cast_chunks(cast_in, cast_out)
    tm = x_ref.shape[0]
    blocks = []
    for r0 in range(0, tm, ATTN_ROWS):
        x = x_ref[r0:r0 + ATTN_ROWS, :]
        n = _rms(x, g_x_ref[...]).astype(_BF16)
        q = _dot(n, w_q_ref[...]).astype(_BF16)
        heads = []
        for hd in range(XA_HEADS):
            cols = slice(hd * XA_HEAD_DIM, (hd + 1) * XA_HEAD_DIM)
            sc = _dot(q[:, cols], kt_ref[cols, :])
            e = jnp.exp(sc - jnp.max(sc, axis=-1, keepdims=True))
            p = (e / jnp.sum(e, axis=-1, keepdims=True)).astype(_BF16)
            heads.append(_dot(p, v_ref[:, cols]))
        o = jnp.concatenate(heads, axis=-1).astype(_BF16)
        blocks.append(x + _dot(o, w_o_ref[...]))
    h = jnp.concatenate(blocks, axis=0)

    n = _rms(h, g_f_ref[...]).astype(_BF16)
    d_ff = w_gate_ref.shape[1]
    for c0 in range(0, d_ff, FF_CHUNK):
        cols = slice(c0, min(c0 + FF_CHUNK, d_ff))
        act = (_silu(_dot(n, w_gate_ref[:, cols])) * _dot(n, w_up_ref[:, cols])).astype(_BF16)
        h = h + _dot(act, w_down_ref[cols, :])
    if final_norm:
        h = _rms(h, g_fin_ref[...])
    o_ref[...] = h


def _xattn_ffn(h, seq, kt, v, g_x, g_f, g_fin, weights, final_norm, next_weights, next_layer):
    m, d = h.shape
    tm = TM_FFN
    assert seq % tm == 0
    tiles_per_seq = seq // tm
    n_tiles = m // tm
    tile = _tile_index(0, n_tiles)
    cast_in, cast_out, cast_shapes = _cast_plan(next_weights, next_layer, n_tiles, tile)
    outs = pl.pallas_call(
        functools.partial(_xattn_ffn_kernel, final_norm, len(next_weights)),
        name="xattn_ffn",
        grid=(n_tiles,),
        in_specs=[
            pl.BlockSpec((tm, d), lambda s: (s, 0)),
            _const_spec((1, d)),
            pl.BlockSpec((None, d, N_MEM), lambda s: (s // tiles_per_seq, 0, 0)),
            pl.BlockSpec((None, N_MEM, d), lambda s: (s // tiles_per_seq, 0, 0)),
            _const_spec((1, d)),
            _const_spec((1, d)),
        ] + [_const_spec(w.shape) for w in weights] + cast_in,
        out_specs=[pl.BlockSpec((tm, d), lambda s: (s, 0))] + cast_out,
        out_shape=[jax.ShapeDtypeStruct((m, d), _F32)] + cast_shapes,
        compiler_params=_params(),
    )(h, g_x, kt, v, g_f, g_fin, *weights, *next_weights)
    return outs[0], outs[1:]


def kernel(x, mem, g_mix, g_xattn, g_mem, g_ffn, g_final, ev_w_in, ev_a_conv_w, ev_a_conv_b, ev_a_ln_g, ev_a_ln_b, ev_b_conv_w, ev_b_conv_b, ev_w_out, od_w_in, od_c_ln_g, od_c_ln_b, od_w_s, od_b_s, od_w_out, xa_w_q, xa_w_k, xa_w_v, xa_w_o, ffn_w_gate, ffn_w_up, ffn_w_down):
    bsz, seq, d = x.shape
    depth = g_mix.shape[0]
    assert depth == 2
    row = lambda p: p.reshape(1, -1)
    xf_stacked = (xa_w_q, xa_w_o, ffn_w_gate, ffn_w_up, ffn_w_down)
    h = x.reshape(bsz * seq, d)

    h, xf_w = _mix0(h, seq, 0, row(g_mix[0]), ev_w_in, ev_a_conv_w, row(ev_a_conv_b[0]),
                    row(ev_a_ln_g[0]), row(ev_a_ln_b[0]), ev_b_conv_w, row(ev_b_conv_b[0]), ev_w_out,
                    next_weights=xf_stacked, next_layer=0)
    kt, v = _kv(mem, 0, row(g_mem[0]), xa_w_k, xa_w_v)
    h, (w_in1, w_out1) = _xattn_ffn(h, seq, kt, v, row(g_xattn[0]), row(g_ffn[0]), row(g_final), xf_w,
                                    final_norm=False, next_weights=(od_w_in, od_w_out), next_layer=0)

    bias = jnp.repeat(od_b_s[0].T, d // C_GROUPS, axis=1)
    h, xf_w = _mix1(h, 0, row(g_mix[1]), w_in1, w_out1, row(od_c_ln_g[0]), row(od_c_ln_b[0]), od_w_s,
                    bias, next_weights=xf_stacked, next_layer=1)
    kt, v = _kv(mem, 1, row(g_mem[1]), xa_w_k, xa_w_v)
    h, _ = _xattn_ffn(h, seq, kt, v, row(g_xattn[1]), row(g_ffn[1]), row(g_final), xf_w,
                      final_norm=True, next_weights=(), next_layer=0)
    return h.reshape(bsz, seq, d)
```

```python
import functools

import jax
import jax.numpy as jnp
from jax import lax
from jax.experimental import pallas as pl
from jax.experimental.pallas import tpu as pltpu

D_MODEL = 1024
N_MEM = 256
D_A = 512
D_B = 512
A_KERNEL = 31
B_KERNEL = 3
CHUNK = 128
C_GROUPS = 8
XA_HEADS = 4
XA_HEAD_DIM = D_MODEL // XA_HEADS
RMS_EPS = 1e-6
LN_EPS = 1e-5

F32_SUBLANES = 8
BF16_SUBLANES = 16
HALO = 16
TM_MIX0 = 512
TM_MIX1 = 1024
TM_FFN = 1024
ATTN_ROWS = 512
FF_CHUNK = 256
IN_PROJ_CHUNK = 512
W_STEPS_MIX0 = 4
VMEM_LIMIT_BYTES = 56 * 1024 * 1024

_BF16 = jnp.bfloat16
_F32 = jnp.float32


def _dot(a, b):
    return jnp.dot(a, b, preferred_element_type=_F32)


def _rms(x, g):
    ms = jnp.mean(x * x, axis=-1, keepdims=True)
    return x * lax.rsqrt(ms + RMS_EPS) * g


def _layer_norm(x, g, b):
    mu = jnp.mean(x, axis=-1, keepdims=True)
    xc = x - mu
    var = jnp.mean(xc * xc, axis=-1, keepdims=True)
    return xc * lax.rsqrt(var + LN_EPS) * g + b


def _sigmoid(x):
    return 1.0 / (1.0 + jnp.exp(-x))


def _silu(x):
    return x * _sigmoid(x)


def _gelu_tanh(x):
    k = -2.0 * 0.7978845608028654 * 1.4426950408889634
    return x / (1.0 + jnp.exp2(x * (k + (k * 0.044715) * (x * x))))


def _const_spec(shape):
    zeros = (0,) * len(shape)
    return pl.BlockSpec(shape, lambda s: zeros, pipeline_mode=pl.Buffered(1))


def _layer_spec(w, layer):
    zeros = (0,) * (w.ndim - 1)
    return pl.BlockSpec((None,) + w.shape[1:], lambda s: (layer,) + zeros,
                        pipeline_mode=pl.Buffered(1))


def _params():
    return pltpu.CompilerParams(
        dimension_semantics=("arbitrary",), vmem_limit_bytes=VMEM_LIMIT_BYTES)


def _tile_index(n_w_steps, n_tiles):
    return lambda s: jnp.clip(s - n_w_steps, 0, n_tiles - 1)


def _cast_plan(stacked_weights, layer, n_tiles, tile):
    in_specs, out_specs, out_shapes = [], [], []
    for w in stacked_weights:
        _, rows, cols = w.shape
        tiles_per_chunk = 1
        while (rows * tiles_per_chunk) % (n_tiles * BF16_SUBLANES) != 0:
            tiles_per_chunk *= 2
        chunk = rows * tiles_per_chunk // n_tiles
        in_specs.append(pl.BlockSpec(
            (None, chunk, cols), lambda s, k=tiles_per_chunk: (layer, tile(s) // k, 0)))
        out_specs.append(pl.BlockSpec(
            (chunk, cols), lambda s, k=tiles_per_chunk: (tile(s) // k, 0)))
        out_shapes.append(jax.ShapeDtypeStruct((rows, cols), _BF16))
    return in_specs, out_specs, out_shapes


def _cast_chunks(in_refs, out_refs):
    for in_ref, out_ref in zip(in_refs, out_refs):
        out_ref[...] = in_ref[...].astype(_BF16)


def _realign_phases(src_scr, shift_scr, first, n_taps):
    rows = src_scr.shape[0]
    phases = sorted({(first + k) % F32_SUBLANES for k in range(n_taps)} - {0})
    for r in phases:
        shift_scr[r - 1, pl.ds(0, rows - F32_SUBLANES), :] = src_scr[pl.ds(r, rows - F32_SUBLANES), :]


def _depthwise_conv(src_scr, shift_scr, w, bias, first, tm):
    rows = src_scr.shape[0]
    acc = bias
    for k in range(w.shape[0]):
        r = (first + k) % F32_SUBLANES
        base = first + k - r
        assert base + tm <= rows - F32_SUBLANES
        if r == 0:
            tap = src_scr[pl.ds(base, tm), :]
        else:
            tap = shift_scr[r - 1, pl.ds(base, tm), :]
        acc = acc + w[k:k + 1, :] * tap
    return acc


def _mix0_kernel(n_w_steps, tiles_per_seq, n_cast, xp_ref, xc_ref, xn_ref, g_ref,
                 w_in_ref, w_out_ref, a_w_ref, a_b_ref, ln_g_ref, ln_b_ref, b_w_ref, b_b_ref, *refs):
    cast_in, o_ref, cast_out = refs[:n_cast], refs[n_cast], refs[n_cast + 1:2 * n_cast + 1]
    w_in_scr, w_out_scr, a_scr, c_scr, shift_scr = refs[2 * n_cast + 1:]
    s = pl.program_id(0)

    @pl.when(s < n_w_steps)
    def _():
        for chunk_ref, scr_ref in ((w_in_ref, w_in_scr), (w_out_ref, w_out_scr)):
            rows = chunk_ref.shape[0]
            start = pl.multiple_of(s * rows, rows)
            scr_ref[pl.ds(start, rows), :] = chunk_ref[...].astype(_BF16)

    @pl.when(s >= n_w_steps)
    def _():
        _cast_chunks(cast_in, cast_out)
        tm = xc_ref.shape[0]
        first_a = HALO - A_KERNEL // 2
        first_b = HALO - B_KERNEL // 2
        t = (s - n_w_steps) % tiles_per_seq
        x_prev = jnp.where(t == 0, 0.0, xp_ref[...])
        x_next = jnp.where(t == tiles_per_seq - 1, 0.0, xn_ref[...])
        xc = xc_ref[...]
        xe = jnp.concatenate([x_prev, xc, x_next], axis=0)
        n = _rms(xe, g_ref[...]).astype(_BF16)
        z = _dot(n, w_in_scr[...])

        a_scr[...] = z[:, :D_A] * _sigmoid(z[:, D_A:2 * D_A])
        _realign_phases(a_scr, shift_scr, first_a, A_KERNEL)
        conv_a = _depthwise_conv(a_scr, shift_scr, a_w_ref[...], a_b_ref[...], first_a, tm)
        a = _silu(_layer_norm(conv_a, ln_g_ref[...], ln_b_ref[...]))

        off_b = 2 * D_A
        c_scr[...] = z[:, off_b + 2 * D_B:off_b + 3 * D_B] * z[:, off_b:off_b + D_B]
        _realign_phases(c_scr, shift_scr, first_b, B_KERNEL)
        conv_b = _depthwise_conv(c_scr, shift_scr, b_w_ref[...], b_b_ref[...], first_b, tm)
        b = z[HALO:HALO + tm, off_b + D_B:off_b + 2 * D_B] * conv_b

        y = jnp.concatenate([a, b], axis=-1).astype(_BF16)
        o_ref[...] = xc + _dot(y, w_out_scr[...])


def _mix0(h, seq, layer, g, w_in, a_w, a_b, ln_g, ln_b, b_w, b_b, w_out, next_weights, next_layer):
    m, d = h.shape
    tm = TM_MIX0
    assert seq % tm == 0 and tm % HALO == 0
    assert D_A == D_B
    n_tiles = m // tm
    n_w = W_STEPS_MIX0
    halo_blocks = tm // HALO
    last_halo_block = m // HALO - 1
    tile = _tile_index(n_w, n_tiles)
    rows_ext = tm + 2 * HALO

    def own_chunk_spec(w):
        rows = w.shape[1] // n_w
        assert rows * n_w == w.shape[1] and rows % BF16_SUBLANES == 0
        return pl.BlockSpec((None, rows, w.shape[2]), lambda s: (layer, jnp.minimum(s, n_w - 1), 0))

    cast_in, cast_out, cast_shapes = _cast_plan(next_weights, next_layer, n_tiles, tile)
    outs = pl.pallas_call(
        functools.partial(_mix0_kernel, n_w, seq // tm, len(next_weights)),
        name="mix0",
        grid=(n_w + n_tiles,),
        in_specs=[
            pl.BlockSpec((HALO, d), lambda s: (jnp.maximum(tile(s) * halo_blocks - 1, 0), 0)),
            pl.BlockSpec((tm, d), lambda s: (tile(s), 0)),
            pl.BlockSpec((HALO, d), lambda s: (jnp.minimum((tile(s) + 1) * halo_blocks, last_halo_block), 0)),
            _const_spec((1, d)),
            own_chunk_spec(w_in),
            own_chunk_spec(w_out),
            _layer_spec(a_w, layer),
            _const_spec((1, D_A)),
            _const_spec((1, D_A)),
            _const_spec((1, D_A)),
            _layer_spec(b_w, layer),
            _const_spec((1, D_B)),
        ] + cast_in,
        out_specs=[pl.BlockSpec((tm, d), lambda s: (tile(s), 0))] + cast_out,
        out_shape=[jax.ShapeDtypeStruct((m, d), _F32)] + cast_shapes,
        scratch_shapes=[pltpu.VMEM(w_in.shape[1:], _BF16), pltpu.VMEM(w_out.shape[1:], _BF16),
                        pltpu.VMEM((rows_ext, D_A), _F32),
                        pltpu.VMEM((rows_ext, D_B), _F32),
                        pltpu.VMEM((F32_SUBLANES - 1, rows_ext, D_A), _F32)],
        compiler_params=_params(),
    )(h, h, h, g, w_in, w_out, a_w, a_b, ln_g, ln_b, b_w, b_b, *next_weights)
    return outs[0], outs[1:]


def _mix1_kernel(n_cast, x_ref, g_ref, w_in_ref, w_out_ref, ln_g_ref, ln_b_ref, w_s_ref,
                 bias_ref, *refs):
    cast_in, o_ref, cast_out = refs[:n_cast], refs[n_cast], refs[n_cast + 1:2 * n_cast + 1]
    (y_scr,) = refs[2 * n_cast + 1:]
    _cast_chunks(cast_in, cast_out)
    tm, d = x_ref.shape
    n_chunks = tm // CHUNK
    gdim = d // C_GROUPS
    x = x_ref[...]
    n = _rms(x, g_ref[...]).astype(_BF16)
    project = lambda c0: _gelu_tanh(_dot(n, w_in_ref[:, c0:c0 + IN_PROJ_CHUNK]))
    v = jnp.concatenate([project(d + c0) for c0 in range(0, d, IN_PROJ_CHUNK)], axis=-1)
    v = _layer_norm(v, ln_g_ref[...], ln_b_ref[...]).astype(_BF16)
    u = jnp.concatenate([project(c0) for c0 in range(0, d, IN_PROJ_CHUNK)], axis=-1)
    bias = bias_ref[...]
    for g in range(C_GROUPS):
        cols = slice(g * gdim, (g + 1) * gdim)
        v_g = jnp.concatenate([v[c * CHUNK:(c + 1) * CHUNK, cols] for c in range(n_chunks)], axis=1)
        sv_g = _dot(w_s_ref[g].astype(_BF16), v_g)
        for c in range(n_chunks):
            rows = slice(c * CHUNK, (c + 1) * CHUNK)
            sv = sv_g[:, c * gdim:(c + 1) * gdim] + bias[:, cols]
            y_scr[rows, cols] = (u[rows, cols] * sv).astype(_BF16)
    o_ref[...] = x + _dot(y_scr[...], w_out_ref[...])


def _mix1(h, layer, g, w_in, w_out, ln_g, ln_b, w_s, bias, next_weights, next_layer):
    m, d = h.shape
    tm = TM_MIX1
    assert tm % CHUNK == 0
    n_tiles = m // tm
    tile = _tile_index(0, n_tiles)
    cast_in, cast_out, cast_shapes = _cast_plan(next_weights, next_layer, n_tiles, tile)
    outs = pl.pallas_call(
        functools.partial(_mix1_kernel, len(next_weights)),
        name="mix1",
        grid=(n_tiles,),
        in_specs=[
            pl.BlockSpec((tm, d), lambda s: (s, 0)),
            _const_spec((1, d)),
            _const_spec(w_in.shape),
            _const_spec(w_out.shape),
            _const_spec((1, d)),
            _const_spec((1, d)),
            _layer_spec(w_s, layer),
            _const_spec((CHUNK, d)),
        ] + cast_in,
        out_specs=[pl.BlockSpec((tm, d), lambda s: (s, 0))] + cast_out,
        out_shape=[jax.ShapeDtypeStruct((m, d), _F32)] + cast_shapes,
        scratch_shapes=[pltpu.VMEM((tm, d), _BF16)],
        compiler_params=_params(),
    )(h, g, w_in, w_out, ln_g, ln_b, w_s, bias, *next_weights)
    return outs[0], outs[1:]


def _kv_kernel(mem_ref, g_ref, w_k_ref, w_v_ref, kt_ref, v_ref):
    bsz, n_mem, d = v_ref.shape
    n = _rms(mem_ref[...], g_ref[...]).astype(_BF16)
    k = _dot(n, w_k_ref[...].astype(_BF16)) * (XA_HEAD_DIM ** -0.5)
    v = _dot(n, w_v_ref[...].astype(_BF16)).astype(_BF16)
    for b in range(bsz):
        rows = slice(b * n_mem, (b + 1) * n_mem)
        kt_ref[b] = k[rows, :].T.astype(_BF16)
        v_ref[b] = v[rows, :]


def _kv(mem, g_mem, w_k, w_v):
    bsz, n_mem, d = mem.shape
    depth = w_k.shape[0]
    per_layer = lambda *block: pl.BlockSpec((None,) + block, lambda l: (l,) + (0,) * len(block))
    return pl.pallas_call(
        _kv_kernel,
        name="kv",
        grid=(depth,),
        in_specs=[_const_spec((bsz * n_mem, d)), per_layer(1, d), per_layer(d, d), per_layer(d, d)],
        out_specs=[per_layer(bsz, d, n_mem), per_layer(bsz, n_mem, d)],
        out_shape=[jax.ShapeDtypeStruct((depth, bsz, d, n_mem), _BF16),
                   jax.ShapeDtypeStruct((depth, bsz, n_mem, d), _BF16)],
        compiler_params=_params(),
    )(mem.reshape(bsz * n_mem, d), g_mem.reshape(depth, 1, d), w_k, w_v)


def _xattn_ffn_kernel(final_norm, n_cast, x_ref, g_x_ref, kt_ref, v_ref, g_f_ref, g_fin_ref,
                      w_q_ref, w_o_ref, w_gate_ref, w_up_ref, w_down_ref, *refs):
    cast_in, o_ref, cast_out = refs[:n_cast], refs[n_cast], refs[n_cast + 1:]
    _cast_chunks(cast_in, cast_out)
    tm = x_ref.shape[0]
    blocks = []
    for r0 in range(0, tm, ATTN_ROWS):
        x = x_ref[r0:r0 + ATTN_ROWS, :]
        n = _rms(x, g_x_ref[...]).astype(_BF16)
        q = _dot(n, w_q_ref[...]).astype(_BF16)
        heads = []
        for hd in range(XA_HEADS):
            cols = slice(hd * XA_HEAD_DIM, (hd + 1) * XA_HEAD_DIM)
            sc = _dot(q[:, cols], kt_ref[cols, :])
            e = jnp.exp(sc - jnp.max(sc, axis=-1, keepdims=True))
            p = (e / jnp.sum(e, axis=-1, keepdims=True)).astype(_BF16)
            heads.append(_dot(p, v_ref[:, cols]))
        o = jnp.concatenate(heads, axis=-1).astype(_BF16)
        blocks.append(x + _dot(o, w_o_ref[...]))
    h = jnp.concatenate(blocks, axis=0)

    n = _rms(h, g_f_ref[...]).astype(_BF16)
    d_ff = w_gate_ref.shape[1]
    for c0 in range(0, d_ff, FF_CHUNK):
        cols = slice(c0, min(c0 + FF_CHUNK, d_ff))
        act = (_silu(_dot(n, w_gate_ref[:, cols])) * _dot(n, w_up_ref[:, cols])).astype(_BF16)
        h = h + _dot(act, w_down_ref[cols, :])
    if final_norm:
        h = _rms(h, g_fin_ref[...])
    o_ref[...] = h


def _xattn_ffn(h, seq, layer, kt, v, g_x, g_f, g_fin, weights, final_norm, next_weights, next_layer):
    m, d = h.shape
    tm = TM_FFN
    assert seq % tm == 0
    tiles_per_seq = seq // tm
    n_tiles = m // tm
    tile = _tile_index(0, n_tiles)
    cast_in, cast_out, cast_shapes = _cast_plan(next_weights, next_layer, n_tiles, tile)
    outs = pl.pallas_call(
        functools.partial(_xattn_ffn_kernel, final_norm, len(next_weights)),
        name="xattn_ffn",
        grid=(n_tiles,),
        in_specs=[
            pl.BlockSpec((tm, d), lambda s: (s, 0)),
            _const_spec((1, d)),
            pl.BlockSpec((None, None, d, N_MEM), lambda s: (layer, s // tiles_per_seq, 0, 0)),
            pl.BlockSpec((None, None, N_MEM, d), lambda s: (layer, s // tiles_per_seq, 0, 0)),
            _const_spec((1, d)),
            _const_spec((1, d)),
        ] + [_const_spec(w.shape) for w in weights] + cast_in,
        out_specs=[pl.BlockSpec((tm, d), lambda s: (s, 0))] + cast_out,
        out_shape=[jax.ShapeDtypeStruct((m, d), _F32)] + cast_shapes,
        compiler_params=_params(),
    )(h, g_x, kt, v, g_f, g_fin, *weights, *next_weights)
    return outs[0], outs[1:]


def kernel(x, mem, g_mix, g_xattn, g_mem, g_ffn, g_final, ev_w_in, ev_a_conv_w, ev_a_conv_b, ev_a_ln_g, ev_a_ln_b, ev_b_conv_w, ev_b_conv_b, ev_w_out, od_w_in, od_c_ln_g, od_c_ln_b, od_w_s, od_b_s, od_w_out, xa_w_q, xa_w_k, xa_w_v, xa_w_o, ffn_w_gate, ffn_w_up, ffn_w_down):
    bsz, seq, d = x.shape
    depth = g_mix.shape[0]
    assert depth == 2
    row = lambda p: p.reshape(1, -1)
    xf_stacked = (xa_w_q, xa_w_o, ffn_w_gate, ffn_w_up, ffn_w_down)
    h = x.reshape(bsz * seq, d)

    h, xf_w = _mix0(h, seq, 0, row(g_mix[0]), ev_w_in, ev_a_conv_w, row(ev_a_conv_b[0]),
                    row(ev_a_ln_g[0]), row(ev_a_ln_b[0]), ev_b_conv_w, row(ev_b_conv_b[0]), ev_w_out,
                    next_weights=xf_stacked, next_layer=0)
    kt, v = _kv(mem, g_mem, xa_w_k, xa_w_v)
    h, (w_in1, w_out1) = _xattn_ffn(h, seq, 0, kt, v, row(g_xattn[0]), row(g_ffn[0]), row(g_final), xf_w,
                                    final_norm=False, next_weights=(od_w_in, od_w_out), next_layer=0)

    bias = jnp.repeat(od_b_s[0].T, d // C_GROUPS, axis=1)
    h, xf_w = _mix1(h, 0, row(g_mix[1]), w_in1, w_out1, row(od_c_ln_g[0]), row(od_c_ln_b[0]), od_w_s,
                    bias, next_weights=xf_stacked, next_layer=1)
    h, _ = _xattn_ffn(h, seq, 1, kt, v, row(g_xattn[1]), row(g_ffn[1]), row(g_final), xf_w,
                      final_norm=True, next_weights=(), next_layer=0)
    return h.reshape(bsz, seq, d)
```

```python
import functools

import jax
import jax.numpy as jnp
from jax import lax
from jax.experimental import pallas as pl
from jax.experimental.pallas import tpu as pltpu

D_MODEL = 1024
N_MEM = 256
D_A = 512
D_B = 512
A_KERNEL = 31
B_KERNEL = 3
CHUNK = 128
C_GROUPS = 8
XA_HEADS = 4
XA_HEAD_DIM = D_MODEL // XA_HEADS
RMS_EPS = 1e-6
LN_EPS = 1e-5

F32_SUBLANES = 8
BF16_SUBLANES = 16
HALO = 16
TM_MIX0 = 512
TM_MIX1 = 1024
TM_FFN = 1024
ATTN_ROWS = 512
FF_CHUNK = 256
IN_PROJ_CHUNK = 512
W_STEPS_MIX0 = 4
VMEM_LIMIT_BYTES = 56 * 1024 * 1024

_BF16 = jnp.bfloat16
_F32 = jnp.float32


def _dot(a, b):
    return jnp.dot(a, b, preferred_element_type=_F32)


def _rms(x, g):
    ms = jnp.mean(x * x, axis=-1, keepdims=True)
    return x * lax.rsqrt(ms + RMS_EPS) * g


def _layer_norm(x, g, b):
    mu = jnp.mean(x, axis=-1, keepdims=True)
    xc = x - mu
    var = jnp.mean(xc * xc, axis=-1, keepdims=True)
    return xc * lax.rsqrt(var + LN_EPS) * g + b


def _sigmoid(x):
    return 1.0 / (1.0 + jnp.exp(-x))


def _silu(x):
    return x * _sigmoid(x)


def _gelu_tanh(x):
    k = -2.0 * 0.7978845608028654 * 1.4426950408889634
    return x / (1.0 + jnp.exp2(x * (k + (k * 0.044715) * (x * x))))


def _const_spec(shape):
    zeros = (0,) * len(shape)
    return pl.BlockSpec(shape, lambda s: zeros, pipeline_mode=pl.Buffered(1))


def _layer_spec(w, layer):
    zeros = (0,) * (w.ndim - 1)
    return pl.BlockSpec((None,) + w.shape[1:], lambda s: (layer,) + zeros,
                        pipeline_mode=pl.Buffered(1))


def _params():
    return pltpu.CompilerParams(
        dimension_semantics=("arbitrary",), vmem_limit_bytes=VMEM_LIMIT_BYTES)


def _tile_index(n_w_steps, n_tiles):
    return lambda s: jnp.clip(s - n_w_steps, 0, n_tiles - 1)


def _cast_plan(stacked_weights, layers, n_tiles, tile):
    in_specs, out_specs, out_shapes = [], [], []
    for w, layer in zip(stacked_weights, layers, strict=True):
        _, rows, cols = w.shape
        tiles_per_chunk = 1
        while (rows * tiles_per_chunk) % (n_tiles * BF16_SUBLANES) != 0:
            tiles_per_chunk *= 2
        chunk = rows * tiles_per_chunk // n_tiles
        in_specs.append(pl.BlockSpec(
            (None, chunk, cols), lambda s, k=tiles_per_chunk, l=layer: (l, tile(s) // k, 0)))
        out_specs.append(pl.BlockSpec(
            (chunk, cols), lambda s, k=tiles_per_chunk: (tile(s) // k, 0)))
        out_shapes.append(jax.ShapeDtypeStruct((rows, cols), _BF16))
    return in_specs, out_specs, out_shapes


def _cast_chunks(in_refs, out_refs):
    for in_ref, out_ref in zip(in_refs, out_refs):
        out_ref[...] = in_ref[...].astype(_BF16)


def _realign_phases(src_scr, shift_scr, first, n_taps):
    rows = src_scr.shape[0]
    phases = sorted({(first + k) % F32_SUBLANES for k in range(n_taps)} - {0})
    for r in phases:
        shift_scr[r - 1, pl.ds(0, rows - F32_SUBLANES), :] = src_scr[pl.ds(r, rows - F32_SUBLANES), :]


def _depthwise_conv(src_scr, shift_scr, w, bias, first, tm):
    rows = src_scr.shape[0]
    acc = bias
    for k in range(w.shape[0]):
        r = (first + k) % F32_SUBLANES
        base = first + k - r
        assert base + tm <= rows - F32_SUBLANES
        if r == 0:
            tap = src_scr[pl.ds(base, tm), :]
        else:
            tap = shift_scr[r - 1, pl.ds(base, tm), :]
        acc = acc + w[k:k + 1, :] * tap
    return acc


def _mix0_kernel(n_w_steps, tiles_per_seq, n_cast, xp_ref, xc_ref, xn_ref, g_ref,
                 w_in_ref, w_out_ref, a_w_ref, a_b_ref, ln_g_ref, ln_b_ref, b_w_ref, b_b_ref, *refs):
    cast_in, o_ref, cast_out = refs[:n_cast], refs[n_cast], refs[n_cast + 1:2 * n_cast + 1]
    w_in_scr, w_out_scr, a_scr, c_scr, shift_scr = refs[2 * n_cast + 1:]
    s = pl.program_id(0)

    @pl.when(s < n_w_steps)
    def _():
        for chunk_ref, scr_ref in ((w_in_ref, w_in_scr), (w_out_ref, w_out_scr)):
            rows = chunk_ref.shape[0]
            start = pl.multiple_of(s * rows, rows)
            scr_ref[pl.ds(start, rows), :] = chunk_ref[...].astype(_BF16)

    @pl.when(s >= n_w_steps)
    def _():
        _cast_chunks(cast_in, cast_out)
        tm = xc_ref.shape[0]
        first_a = HALO - A_KERNEL // 2
        first_b = HALO - B_KERNEL // 2
        t = (s - n_w_steps) % tiles_per_seq
        x_prev = jnp.where(t == 0, 0.0, xp_ref[...])
        x_next = jnp.where(t == tiles_per_seq - 1, 0.0, xn_ref[...])
        xc = xc_ref[...]
        xe = jnp.concatenate([x_prev, xc, x_next], axis=0)
        n = _rms(xe, g_ref[...]).astype(_BF16)
        z = _dot(n, w_in_scr[...])

        a_scr[...] = z[:, :D_A] * _sigmoid(z[:, D_A:2 * D_A])
        _realign_phases(a_scr, shift_scr, first_a, A_KERNEL)
        conv_a = _depthwise_conv(a_scr, shift_scr, a_w_ref[...], a_b_ref[...], first_a, tm)
        a = _silu(_layer_norm(conv_a, ln_g_ref[...], ln_b_ref[...]))

        off_b = 2 * D_A
        c_scr[...] = z[:, off_b + 2 * D_B:off_b + 3 * D_B] * z[:, off_b:off_b + D_B]
        _realign_phases(c_scr, shift_scr, first_b, B_KERNEL)
        conv_b = _depthwise_conv(c_scr, shift_scr, b_w_ref[...], b_b_ref[...], first_b, tm)
        b = z[HALO:HALO + tm, off_b + D_B:off_b + 2 * D_B] * conv_b

        y = jnp.concatenate([a, b], axis=-1).astype(_BF16)
        o_ref[...] = xc + _dot(y, w_out_scr[...])


def _mix0(h, seq, layer, g, w_in, a_w, a_b, ln_g, ln_b, b_w, b_b, w_out, next_weights, next_layers):
    m, d = h.shape
    tm = TM_MIX0
    assert seq % tm == 0 and tm % HALO == 0
    assert D_A == D_B
    n_tiles = m // tm
    n_w = W_STEPS_MIX0
    halo_blocks = tm // HALO
    last_halo_block = m // HALO - 1
    tile = _tile_index(n_w, n_tiles)
    rows_ext = tm + 2 * HALO

    def own_chunk_spec(w):
        rows = w.shape[1] // n_w
        assert rows * n_w == w.shape[1] and rows % BF16_SUBLANES == 0
        return pl.BlockSpec((None, rows, w.shape[2]), lambda s: (layer, jnp.minimum(s, n_w - 1), 0))

    cast_in, cast_out, cast_shapes = _cast_plan(next_weights, next_layers, n_tiles, tile)
    outs = pl.pallas_call(
        functools.partial(_mix0_kernel, n_w, seq // tm, len(next_weights)),
        name="mix0",
        grid=(n_w + n_tiles,),
        in_specs=[
            pl.BlockSpec((HALO, d), lambda s: (jnp.maximum(tile(s) * halo_blocks - 1, 0), 0)),
            pl.BlockSpec((tm, d), lambda s: (tile(s), 0)),
            pl.BlockSpec((HALO, d), lambda s: (jnp.minimum((tile(s) + 1) * halo_blocks, last_halo_block), 0)),
            _const_spec((1, d)),
            own_chunk_spec(w_in),
            own_chunk_spec(w_out),
            _layer_spec(a_w, layer),
            _const_spec((1, D_A)),
            _const_spec((1, D_A)),
            _const_spec((1, D_A)),
            _layer_spec(b_w, layer),
            _const_spec((1, D_B)),
        ] + cast_in,
        out_specs=[pl.BlockSpec((tm, d), lambda s: (tile(s), 0))] + cast_out,
        out_shape=[jax.ShapeDtypeStruct((m, d), _F32)] + cast_shapes,
        scratch_shapes=[pltpu.VMEM(w_in.shape[1:], _BF16), pltpu.VMEM(w_out.shape[1:], _BF16),
                        pltpu.VMEM((rows_ext, D_A), _F32),
                        pltpu.VMEM((rows_ext, D_B), _F32),
                        pltpu.VMEM((F32_SUBLANES - 1, rows_ext, D_A), _F32)],
        compiler_params=_params(),
    )(h, h, h, g, w_in, w_out, a_w, a_b, ln_g, ln_b, b_w, b_b, *next_weights)
    return outs[0], outs[1:]


def _mix1_kernel(n_cast, x_ref, g_ref, w_in_ref, w_out_ref, ln_g_ref, ln_b_ref, w_s_ref,
                 bias_ref, *refs):
    cast_in, o_ref, cast_out = refs[:n_cast], refs[n_cast], refs[n_cast + 1:2 * n_cast + 1]
    (y_scr,) = refs[2 * n_cast + 1:]
    _cast_chunks(cast_in, cast_out)
    tm, d = x_ref.shape
    n_chunks = tm // CHUNK
    gdim = d // C_GROUPS
    x = x_ref[...]
    n = _rms(x, g_ref[...]).astype(_BF16)
    project = lambda c0: _gelu_tanh(_dot(n, w_in_ref[:, c0:c0 + IN_PROJ_CHUNK]))
    v = jnp.concatenate([project(d + c0) for c0 in range(0, d, IN_PROJ_CHUNK)], axis=-1)
    v = _layer_norm(v, ln_g_ref[...], ln_b_ref[...]).astype(_BF16)
    u = jnp.concatenate([project(c0) for c0 in range(0, d, IN_PROJ_CHUNK)], axis=-1)
    bias = bias_ref[...]
    for g in range(C_GROUPS):
        cols = slice(g * gdim, (g + 1) * gdim)
        v_g = jnp.concatenate([v[c * CHUNK:(c + 1) * CHUNK, cols] for c in range(n_chunks)], axis=1)
        sv_g = _dot(w_s_ref[g].astype(_BF16), v_g)
        for c in range(n_chunks):
            rows = slice(c * CHUNK, (c + 1) * CHUNK)
            sv = sv_g[:, c * gdim:(c + 1) * gdim] + bias[:, cols]
            y_scr[rows, cols] = (u[rows, cols] * sv).astype(_BF16)
    o_ref[...] = x + _dot(y_scr[...], w_out_ref[...])


def _mix1(h, layer, g, w_in, w_out, ln_g, ln_b, w_s, bias, next_weights, next_layers):
    m, d = h.shape
    tm = TM_MIX1
    assert tm % CHUNK == 0
    n_tiles = m // tm
    tile = _tile_index(0, n_tiles)
    cast_in, cast_out, cast_shapes = _cast_plan(next_weights, next_layers, n_tiles, tile)
    outs = pl.pallas_call(
        functools.partial(_mix1_kernel, len(next_weights)),
        name="mix1",
        grid=(n_tiles,),
        in_specs=[
            pl.BlockSpec((tm, d), lambda s: (s, 0)),
            _const_spec((1, d)),
            _const_spec(w_in.shape),
            _const_spec(w_out.shape),
            _const_spec((1, d)),
            _const_spec((1, d)),
            _layer_spec(w_s, layer),
            _const_spec((CHUNK, d)),
        ] + cast_in,
        out_specs=[pl.BlockSpec((tm, d), lambda s: (s, 0))] + cast_out,
        out_shape=[jax.ShapeDtypeStruct((m, d), _F32)] + cast_shapes,
        scratch_shapes=[pltpu.VMEM((tm, d), _BF16)],
        compiler_params=_params(),
    )(h, g, w_in, w_out, ln_g, ln_b, w_s, bias, *next_weights)
    return outs[0], outs[1:]


def _kv_kernel(mem_ref, g_ref, w_k_ref, w_v_ref, kt_ref, v_ref):
    bsz, n_mem, d = v_ref.shape
    n = _rms(mem_ref[...], g_ref[...]).astype(_BF16)
    k = _dot(n, w_k_ref[...].astype(_BF16)) * (XA_HEAD_DIM ** -0.5)
    v = _dot(n, w_v_ref[...].astype(_BF16)).astype(_BF16)
    for b in range(bsz):
        rows = slice(b * n_mem, (b + 1) * n_mem)
        kt_ref[b] = k[rows, :].T.astype(_BF16)
        v_ref[b] = v[rows, :]


def _kv(mem, g_mem, w_k, w_v):
    bsz, n_mem, d = mem.shape
    depth = w_k.shape[0]
    per_layer = lambda *block: pl.BlockSpec((None,) + block, lambda l: (l,) + (0,) * len(block))
    return pl.pallas_call(
        _kv_kernel,
        name="kv",
        grid=(depth,),
        in_specs=[_const_spec((bsz * n_mem, d)), per_layer(1, d), per_layer(d, d), per_layer(d, d)],
        out_specs=[per_layer(bsz, d, n_mem), per_layer(bsz, n_mem, d)],
        out_shape=[jax.ShapeDtypeStruct((depth, bsz, d, n_mem), _BF16),
                   jax.ShapeDtypeStruct((depth, bsz, n_mem, d), _BF16)],
        compiler_params=_params(),
    )(mem.reshape(bsz * n_mem, d), g_mem.reshape(depth, 1, d), w_k, w_v)


def _xattn_ffn_kernel(final_norm, n_cast, x_ref, g_x_ref, kt_ref, v_ref, g_f_ref, g_fin_ref,
                      w_q_ref, w_o_ref, w_gate_ref, w_up_ref, w_down_ref, *refs):
    cast_in, o_ref, cast_out = refs[:n_cast], refs[n_cast], refs[n_cast + 1:]
    _cast_chunks(cast_in, cast_out)
    tm = x_ref.shape[0]
    blocks = []
    for r0 in range(0, tm, ATTN_ROWS):
        x = x_ref[r0:r0 + ATTN_ROWS, :]
        n = _rms(x, g_x_ref[...]).astype(_BF16)
        q = _dot(n, w_q_ref[...]).astype(_BF16)
        heads = []
        for hd in range(XA_HEADS):
            cols = slice(hd * XA_HEAD_DIM, (hd + 1) * XA_HEAD_DIM)
            sc = _dot(q[:, cols], kt_ref[cols, :])
            e = jnp.exp(sc - jnp.max(sc, axis=-1, keepdims=True))
            p = (e / jnp.sum(e, axis=-1, keepdims=True)).astype(_BF16)
            heads.append(_dot(p, v_ref[:, cols]))
        o = jnp.concatenate(heads, axis=-1).astype(_BF16)
        blocks.append(x + _dot(o, w_o_ref[...]))
    h = jnp.concatenate(blocks, axis=0)

    n = _rms(h, g_f_ref[...]).astype(_BF16)
    d_ff = w_gate_ref.shape[1]
    for c0 in range(0, d_ff, FF_CHUNK):
        cols = slice(c0, min(c0 + FF_CHUNK, d_ff))
        act = (_silu(_dot(n, w_gate_ref[:, cols])) * _dot(n, w_up_ref[:, cols])).astype(_BF16)
        h = h + _dot(act, w_down_ref[cols, :])
    if final_norm:
        h = _rms(h, g_fin_ref[...])
    o_ref[...] = h


def _xattn_ffn(h, seq, layer, kt, v, g_x, g_f, g_fin, weights, final_norm, next_weights, next_layers):
    m, d = h.shape
    tm = TM_FFN
    assert seq % tm == 0
    tiles_per_seq = seq // tm
    n_tiles = m // tm
    tile = _tile_index(0, n_tiles)
    cast_in, cast_out, cast_shapes = _cast_plan(next_weights, next_layers, n_tiles, tile)
    outs = pl.pallas_call(
        functools.partial(_xattn_ffn_kernel, final_norm, len(next_weights)),
        name="xattn_ffn",
        grid=(n_tiles,),
        in_specs=[
            pl.BlockSpec((tm, d), lambda s: (s, 0)),
            _const_spec((1, d)),
            pl.BlockSpec((None, None, d, N_MEM), lambda s: (layer, s // tiles_per_seq, 0, 0)),
            pl.BlockSpec((None, None, N_MEM, d), lambda s: (layer, s // tiles_per_seq, 0, 0)),
            _const_spec((1, d)),
            _const_spec((1, d)),
        ] + [_const_spec(w.shape) for w in weights] + cast_in,
        out_specs=[pl.BlockSpec((tm, d), lambda s: (s, 0))] + cast_out,
        out_shape=[jax.ShapeDtypeStruct((m, d), _F32)] + cast_shapes,
        compiler_params=_params(),
    )(h, g_x, kt, v, g_f, g_fin, *weights, *next_weights)
    return outs[0], outs[1:]


def kernel(x, mem, g_mix, g_xattn, g_mem, g_ffn, g_final, ev_w_in, ev_a_conv_w, ev_a_conv_b, ev_a_ln_g, ev_a_ln_b, ev_b_conv_w, ev_b_conv_b, ev_w_out, od_w_in, od_c_ln_g, od_c_ln_b, od_w_s, od_b_s, od_w_out, xa_w_q, xa_w_k, xa_w_v, xa_w_o, ffn_w_gate, ffn_w_up, ffn_w_down):
    bsz, seq, d = x.shape
    depth = g_mix.shape[0]
    assert depth == 2
    row = lambda p: p.reshape(1, -1)
    xf_stacked = (xa_w_q, xa_w_o, ffn_w_gate, ffn_w_up, ffn_w_down)
    h = x.reshape(bsz * seq, d)

    n_xf = len(xf_stacked)
    h, xf_w = _mix0(h, seq, 0, row(g_mix[0]), ev_w_in, ev_a_conv_w, row(ev_a_conv_b[0]),
                    row(ev_a_ln_g[0]), row(ev_a_ln_b[0]), ev_b_conv_w, row(ev_b_conv_b[0]), ev_w_out,
                    next_weights=xf_stacked * depth, next_layers=(0,) * n_xf + (1,) * n_xf)
    kt, v = _kv(mem, g_mem, xa_w_k, xa_w_v)
    h, (w_in1, w_out1) = _xattn_ffn(h, seq, 0, kt, v, row(g_xattn[0]), row(g_ffn[0]), row(g_final),
                                    xf_w[:n_xf], final_norm=False,
                                    next_weights=(od_w_in, od_w_out), next_layers=(0, 0))

    bias = jnp.repeat(od_b_s[0].T, d // C_GROUPS, axis=1)
    h, _ = _mix1(h, 0, row(g_mix[1]), w_in1, w_out1, row(od_c_ln_g[0]), row(od_c_ln_b[0]), od_w_s,
                 bias, next_weights=(), next_layers=())
    h, _ = _xattn_ffn(h, seq, 1, kt, v, row(g_xattn[1]), row(g_ffn[1]), row(g_final), xf_w[n_xf:],
                      final_norm=True, next_weights=(), next_layers=())
    return h.reshape(bsz, seq, d)
```

```python
import functools

import jax
import jax.numpy as jnp
from jax import lax
from jax.experimental import pallas as pl
from jax.experimental.pallas import tpu as pltpu

D_MODEL = 1024
N_MEM = 256
D_A = 512
D_B = 512
A_KERNEL = 31
B_KERNEL = 3
CHUNK = 128
C_GROUPS = 8
XA_HEADS = 4
XA_HEAD_DIM = D_MODEL // XA_HEADS
RMS_EPS = 1e-6
LN_EPS = 1e-5

F32_SUBLANES = 8
BF16_SUBLANES = 16
HALO = 16
TM_MIX0 = 512
TM_MIX1 = 1024
TM_FFN = 1024
ATTN_ROWS = 512
FF_CHUNK = 256
IN_PROJ_CHUNK = 512
W_STEPS_MIX0 = 4
VMEM_LIMIT_BYTES = 56 * 1024 * 1024

_BF16 = jnp.bfloat16
_F32 = jnp.float32


def _dot(a, b):
    return jnp.dot(a, b, preferred_element_type=_F32)


def _rms(x, g):
    ms = jnp.mean(x * x, axis=-1, keepdims=True)
    return x * lax.rsqrt(ms + RMS_EPS) * g


def _layer_norm(x, g, b):
    mu = jnp.mean(x, axis=-1, keepdims=True)
    xc = x - mu
    var = jnp.mean(xc * xc, axis=-1, keepdims=True)
    return xc * lax.rsqrt(var + LN_EPS) * g + b


def _sigmoid(x):
    return 1.0 / (1.0 + jnp.exp(-x))


def _silu(x):
    return x * _sigmoid(x)


def _gelu_tanh(x):
    k = -2.0 * 0.7978845608028654 * 1.4426950408889634
    return x / (1.0 + jnp.exp2(x * (k + (k * 0.044715) * (x * x))))


def _const_spec(shape):
    zeros = (0,) * len(shape)
    return pl.BlockSpec(shape, lambda s: zeros, pipeline_mode=pl.Buffered(1))


def _layer_spec(w, layer):
    zeros = (0,) * (w.ndim - 1)
    return pl.BlockSpec((None,) + w.shape[1:], lambda s: (layer,) + zeros,
                        pipeline_mode=pl.Buffered(1))


def _params():
    return pltpu.CompilerParams(
        dimension_semantics=("arbitrary",), vmem_limit_bytes=VMEM_LIMIT_BYTES)


def _tile_index(n_w_steps, n_tiles):
    return lambda s: jnp.clip(s - n_w_steps, 0, n_tiles - 1)


def _cast_plan(stacked_weights, layers, n_tiles, tile):
    in_specs, out_specs, out_shapes = [], [], []
    for w, layer in zip(stacked_weights, layers, strict=True):
        _, rows, cols = w.shape
        tiles_per_chunk = 1
        while (rows * tiles_per_chunk) % (n_tiles * BF16_SUBLANES) != 0:
            tiles_per_chunk *= 2
        chunk = rows * tiles_per_chunk // n_tiles
        in_specs.append(pl.BlockSpec(
            (None, chunk, cols), lambda s, k=tiles_per_chunk, l=layer: (l, tile(s) // k, 0)))
        out_specs.append(pl.BlockSpec(
            (chunk, cols), lambda s, k=tiles_per_chunk: (tile(s) // k, 0)))
        out_shapes.append(jax.ShapeDtypeStruct((rows, cols), _BF16))
    return in_specs, out_specs, out_shapes


def _cast_chunks(in_refs, out_refs):
    for in_ref, out_ref in zip(in_refs, out_refs):
        out_ref[...] = in_ref[...].astype(_BF16)


def _realign_phases(src_scr, shift_scr, first, n_taps):
    rows = src_scr.shape[0]
    phases = sorted({(first + k) % F32_SUBLANES for k in range(n_taps)} - {0})
    for r in phases:
        shift_scr[r - 1, pl.ds(0, rows - F32_SUBLANES), :] = src_scr[pl.ds(r, rows - F32_SUBLANES), :]


def _depthwise_conv(src_scr, shift_scr, w, bias, first, tm):
    rows = src_scr.shape[0]
    acc = bias
    for k in range(w.shape[0]):
        r = (first + k) % F32_SUBLANES
        base = first + k - r
        assert base + tm <= rows - F32_SUBLANES
        if r == 0:
            tap = src_scr[pl.ds(base, tm), :]
        else:
            tap = shift_scr[r - 1, pl.ds(base, tm), :]
        acc = acc + w[k:k + 1, :] * tap
    return acc


def _mix0_kernel(n_w_steps, tiles_per_seq, n_cast, xp_ref, xc_ref, xn_ref, g_ref,
                 w_in_ref, w_out_ref, a_w_ref, a_b_ref, ln_g_ref, ln_b_ref, b_w_ref, b_b_ref, *refs):
    cast_in, o_ref, cast_out = refs[:n_cast], refs[n_cast], refs[n_cast + 1:2 * n_cast + 1]
    w_in_scr, w_out_scr, a_scr, c_scr, shift_scr = refs[2 * n_cast + 1:]
    s = pl.program_id(0)

    @pl.when(s < n_w_steps)
    def _():
        for chunk_ref, scr_ref in ((w_in_ref, w_in_scr), (w_out_ref, w_out_scr)):
            rows = chunk_ref.shape[0]
            start = pl.multiple_of(s * rows, rows)
            scr_ref[pl.ds(start, rows), :] = chunk_ref[...].astype(_BF16)

    @pl.when(s >= n_w_steps)
    def _():
        _cast_chunks(cast_in, cast_out)
        tm = xc_ref.shape[0]
        first_a = HALO - A_KERNEL // 2
        first_b = HALO - B_KERNEL // 2
        t = (s - n_w_steps) % tiles_per_seq
        x_prev = jnp.where(t == 0, 0.0, xp_ref[...])
        x_next = jnp.where(t == tiles_per_seq - 1, 0.0, xn_ref[...])
        xc = xc_ref[...]
        xe = jnp.concatenate([x_prev, xc, x_next], axis=0)
        n = _rms(xe, g_ref[...]).astype(_BF16)
        z = _dot(n, w_in_scr[...])

        a_scr[...] = z[:, :D_A] * _sigmoid(z[:, D_A:2 * D_A])
        _realign_phases(a_scr, shift_scr, first_a, A_KERNEL)
        conv_a = _depthwise_conv(a_scr, shift_scr, a_w_ref[...], a_b_ref[...], first_a, tm)
        a = _silu(_layer_norm(conv_a, ln_g_ref[...], ln_b_ref[...]))

        off_b = 2 * D_A
        c_scr[...] = z[:, off_b + 2 * D_B:off_b + 3 * D_B] * z[:, off_b:off_b + D_B]
        _realign_phases(c_scr, shift_scr, first_b, B_KERNEL)
        conv_b = _depthwise_conv(c_scr, shift_scr, b_w_ref[...], b_b_ref[...], first_b, tm)
        b = z[HALO:HALO + tm, off_b + D_B:off_b + 2 * D_B] * conv_b

        y = jnp.concatenate([a, b], axis=-1).astype(_BF16)
        o_ref[...] = xc + _dot(y, w_out_scr[...])


def _mix0(h, seq, layer, g, w_in, a_w, a_b, ln_g, ln_b, b_w, b_b, w_out, next_weights, next_layers):
    m, d = h.shape
    tm = TM_MIX0
    assert seq % tm == 0 and tm % HALO == 0
    assert D_A == D_B
    n_tiles = m // tm
    n_w = W_STEPS_MIX0
    halo_blocks = tm // HALO
    last_halo_block = m // HALO - 1
    tile = _tile_index(n_w, n_tiles)
    rows_ext = tm + 2 * HALO

    def own_chunk_spec(w):
        rows = w.shape[1] // n_w
        assert rows * n_w == w.shape[1] and rows % BF16_SUBLANES == 0
        return pl.BlockSpec((None, rows, w.shape[2]), lambda s: (layer, jnp.minimum(s, n_w - 1), 0))

    cast_in, cast_out, cast_shapes = _cast_plan(next_weights, next_layers, n_tiles, tile)
    outs = pl.pallas_call(
        functools.partial(_mix0_kernel, n_w, seq // tm, len(next_weights)),
        name="mix0",
        grid=(n_w + n_tiles,),
        in_specs=[
            pl.BlockSpec((HALO, d), lambda s: (jnp.maximum(tile(s) * halo_blocks - 1, 0), 0)),
            pl.BlockSpec((tm, d), lambda s: (tile(s), 0)),
            pl.BlockSpec((HALO, d), lambda s: (jnp.minimum((tile(s) + 1) * halo_blocks, last_halo_block), 0)),
            _const_spec((1, d)),
            own_chunk_spec(w_in),
            own_chunk_spec(w_out),
            _layer_spec(a_w, layer),
            _const_spec((1, D_A)),
            _const_spec((1, D_A)),
            _const_spec((1, D_A)),
            _layer_spec(b_w, layer),
            _const_spec((1, D_B)),
        ] + cast_in,
        out_specs=[pl.BlockSpec((tm, d), lambda s: (tile(s), 0))] + cast_out,
        out_shape=[jax.ShapeDtypeStruct((m, d), _F32)] + cast_shapes,
        scratch_shapes=[pltpu.VMEM(w_in.shape[1:], _BF16), pltpu.VMEM(w_out.shape[1:], _BF16),
                        pltpu.VMEM((rows_ext, D_A), _F32),
                        pltpu.VMEM((rows_ext, D_B), _F32),
                        pltpu.VMEM((F32_SUBLANES - 1, rows_ext, D_A), _F32)],
        compiler_params=_params(),
    )(h, h, h, g, w_in, w_out, a_w, a_b, ln_g, ln_b, b_w, b_b, *next_weights)
    return outs[0], outs[1:]


def _mix1_kernel(n_cast, x_ref, g_ref, w_in_ref, w_out_ref, ln_g_ref, ln_b_ref, w_s_ref,
                 bias_ref, *refs):
    cast_in, o_ref, cast_out = refs[:n_cast], refs[n_cast], refs[n_cast + 1:2 * n_cast + 1]
    (y_scr,) = refs[2 * n_cast + 1:]
    _cast_chunks(cast_in, cast_out)
    tm, d = x_ref.shape
    n_chunks = tm // CHUNK
    gdim = d // C_GROUPS
    x = x_ref[...]
    n = _rms(x, g_ref[...]).astype(_BF16)
    project = lambda c0: _gelu_tanh(_dot(n, w_in_ref[:, c0:c0 + IN_PROJ_CHUNK]))
    v = jnp.concatenate([project(d + c0) for c0 in range(0, d, IN_PROJ_CHUNK)], axis=-1)
    v = _layer_norm(v, ln_g_ref[...], ln_b_ref[...]).astype(_BF16)
    u = jnp.concatenate([project(c0) for c0 in range(0, d, IN_PROJ_CHUNK)], axis=-1)
    bias = bias_ref[...]
    for g in range(C_GROUPS):
        cols = slice(g * gdim, (g + 1) * gdim)
        v_g = jnp.concatenate([v[c * CHUNK:(c + 1) * CHUNK, cols] for c in range(n_chunks)], axis=1)
        sv_g = _dot(w_s_ref[g].astype(_BF16), v_g)
        for c in range(n_chunks):
            rows = slice(c * CHUNK, (c + 1) * CHUNK)
            sv = sv_g[:, c * gdim:(c + 1) * gdim] + bias[:, cols]
            y_scr[rows, cols] = (u[rows, cols] * sv).astype(_BF16)
    o_ref[...] = x + _dot(y_scr[...], w_out_ref[...])


def _mix1(h, layer, g, w_in, w_out, ln_g, ln_b, w_s, bias, next_weights, next_layers):
    m, d = h.shape
    tm = TM_MIX1
    assert tm % CHUNK == 0
    n_tiles = m // tm
    tile = _tile_index(0, n_tiles)
    cast_in, cast_out, cast_shapes = _cast_plan(next_weights, next_layers, n_tiles, tile)
    outs = pl.pallas_call(
        functools.partial(_mix1_kernel, len(next_weights)),
        name="mix1",
        grid=(n_tiles,),
        in_specs=[
            pl.BlockSpec((tm, d), lambda s: (s, 0)),
            _const_spec((1, d)),
            _const_spec(w_in.shape),
            _const_spec(w_out.shape),
            _const_spec((1, d)),
            _const_spec((1, d)),
            _layer_spec(w_s, layer),
            _const_spec((CHUNK, d)),
        ] + cast_in,
        out_specs=[pl.BlockSpec((tm, d), lambda s: (s, 0))] + cast_out,
        out_shape=[jax.ShapeDtypeStruct((m, d), _F32)] + cast_shapes,
        scratch_shapes=[pltpu.VMEM((tm, d), _BF16)],
        compiler_params=_params(),
    )(h, g, w_in, w_out, ln_g, ln_b, w_s, bias, *next_weights)
    return outs[0], outs[1:]


def _kv_kernel(mem_ref, g_ref, w_k_ref, w_v_ref, kt_ref, v_ref):
    bsz, n_mem, d = v_ref.shape
    n = _rms(mem_ref[...], g_ref[...]).astype(_BF16)
    k = _dot(n, w_k_ref[...].astype(_BF16)) * (XA_HEAD_DIM ** -0.5)
    v = _dot(n, w_v_ref[...].astype(_BF16)).astype(_BF16)
    for b in range(bsz):
        rows = slice(b * n_mem, (b + 1) * n_mem)
        kt_ref[b] = k[rows, :].T.astype(_BF16)
        v_ref[b] = v[rows, :]


def _kv(mem, g_mem, w_k, w_v):
    bsz, n_mem, d = mem.shape
    depth = w_k.shape[0]
    per_layer = lambda *block: pl.BlockSpec((None,) + block, lambda l: (l,) + (0,) * len(block))
    return pl.pallas_call(
        _kv_kernel,
        name="kv",
        grid=(depth,),
        in_specs=[_const_spec((bsz * n_mem, d)), per_layer(1, d), per_layer(d, d), per_layer(d, d)],
        out_specs=[per_layer(bsz, d, n_mem), per_layer(bsz, n_mem, d)],
        out_shape=[jax.ShapeDtypeStruct((depth, bsz, d, n_mem), _BF16),
                   jax.ShapeDtypeStruct((depth, bsz, n_mem, d), _BF16)],
        compiler_params=_params(),
    )(mem.reshape(bsz * n_mem, d), g_mem.reshape(depth, 1, d), w_k, w_v)


def _xattn_ffn_kernel(final_norm, n_cast, x_ref, g_x_ref, kt_ref, v_ref, g_f_ref, g_fin_ref,
                      w_q_ref, w_o_ref, w_gate_ref, w_up_ref, w_down_ref, *refs):
    cast_in, o_ref, cast_out = refs[:n_cast], refs[n_cast], refs[n_cast + 1:2 * n_cast + 1]
    (act_scr,) = refs[2 * n_cast + 1:]
    _cast_chunks(cast_in, cast_out)
    tm = x_ref.shape[0]
    blocks = []
    for r0 in range(0, tm, ATTN_ROWS):
        x = x_ref[r0:r0 + ATTN_ROWS, :]
        n = _rms(x, g_x_ref[...]).astype(_BF16)
        q = _dot(n, w_q_ref[...]).astype(_BF16)
        heads = []
        for hd in range(XA_HEADS):
            cols = slice(hd * XA_HEAD_DIM, (hd + 1) * XA_HEAD_DIM)
            sc = _dot(q[:, cols], kt_ref[cols, :])
            e = jnp.exp(sc - jnp.max(sc, axis=-1, keepdims=True))
            p = (e / jnp.sum(e, axis=-1, keepdims=True)).astype(_BF16)
            heads.append(_dot(p, v_ref[:, cols]))
        o = jnp.concatenate(heads, axis=-1).astype(_BF16)
        blocks.append(x + _dot(o, w_o_ref[...]))
    h = jnp.concatenate(blocks, axis=0)

    n = _rms(h, g_f_ref[...]).astype(_BF16)
    d_ff = w_gate_ref.shape[1]
    for c0 in range(0, d_ff, FF_CHUNK):
        cols = slice(c0, min(c0 + FF_CHUNK, d_ff))
        act = _silu(_dot(n, w_gate_ref[:, cols])) * _dot(n, w_up_ref[:, cols])
        act_scr[:, cols] = act.astype(_BF16)
    h = h + _dot(act_scr[...], w_down_ref[...])
    if final_norm:
        h = _rms(h, g_fin_ref[...])
    o_ref[...] = h


def _xattn_ffn(h, seq, layer, kt, v, g_x, g_f, g_fin, weights, final_norm, next_weights, next_layers):
    m, d = h.shape
    tm = TM_FFN
    assert seq % tm == 0
    tiles_per_seq = seq // tm
    n_tiles = m // tm
    tile = _tile_index(0, n_tiles)
    cast_in, cast_out, cast_shapes = _cast_plan(next_weights, next_layers, n_tiles, tile)
    outs = pl.pallas_call(
        functools.partial(_xattn_ffn_kernel, final_norm, len(next_weights)),
        name="xattn_ffn",
        grid=(n_tiles,),
        in_specs=[
            pl.BlockSpec((tm, d), lambda s: (s, 0)),
            _const_spec((1, d)),
            pl.BlockSpec((None, None, d, N_MEM), lambda s: (layer, s // tiles_per_seq, 0, 0)),
            pl.BlockSpec((None, None, N_MEM, d), lambda s: (layer, s // tiles_per_seq, 0, 0)),
            _const_spec((1, d)),
            _const_spec((1, d)),
        ] + [_const_spec(w.shape) for w in weights] + cast_in,
        out_specs=[pl.BlockSpec((tm, d), lambda s: (s, 0))] + cast_out,
        out_shape=[jax.ShapeDtypeStruct((m, d), _F32)] + cast_shapes,
        scratch_shapes=[pltpu.VMEM((tm, weights[2].shape[1]), _BF16)],
        compiler_params=_params(),
    )(h, g_x, kt, v, g_f, g_fin, *weights, *next_weights)
    return outs[0], outs[1:]


def kernel(x, mem, g_mix, g_xattn, g_mem, g_ffn, g_final, ev_w_in, ev_a_conv_w, ev_a_conv_b, ev_a_ln_g, ev_a_ln_b, ev_b_conv_w, ev_b_conv_b, ev_w_out, od_w_in, od_c_ln_g, od_c_ln_b, od_w_s, od_b_s, od_w_out, xa_w_q, xa_w_k, xa_w_v, xa_w_o, ffn_w_gate, ffn_w_up, ffn_w_down):
    bsz, seq, d = x.shape
    depth = g_mix.shape[0]
    assert depth == 2
    row = lambda p: p.reshape(1, -1)
    xf_stacked = (xa_w_q, xa_w_o, ffn_w_gate, ffn_w_up, ffn_w_down)
    h = x.reshape(bsz * seq, d)

    n_xf = len(xf_stacked)
    h, xf_w = _mix0(h, seq, 0, row(g_mix[0]), ev_w_in, ev_a_conv_w, row(ev_a_conv_b[0]),
                    row(ev_a_ln_g[0]), row(ev_a_ln_b[0]), ev_b_conv_w, row(ev_b_conv_b[0]), ev_w_out,
                    next_weights=xf_stacked * depth, next_layers=(0,) * n_xf + (1,) * n_xf)
    kt, v = _kv(mem, g_mem, xa_w_k, xa_w_v)
    h, (w_in1, w_out1) = _xattn_ffn(h, seq, 0, kt, v, row(g_xattn[0]), row(g_ffn[0]), row(g_final),
                                    xf_w[:n_xf], final_norm=False,
                                    next_weights=(od_w_in, od_w_out), next_layers=(0, 0))

    bias = jnp.repeat(od_b_s[0].T, d // C_GROUPS, axis=1)
    h, _ = _mix1(h, 0, row(g_mix[1]), w_in1, w_out1, row(od_c_ln_g[0]), row(od_c_ln_b[0]), od_w_s,
                 bias, next_weights=(), next_layers=())
    h, _ = _xattn_ffn(h, seq, 1, kt, v, row(g_xattn[1]), row(g_ffn[1]), row(g_final), xf_w[n_xf:],
                      final_norm=True, next_weights=(), next_layers=())
    return h.reshape(bsz, seq, d)
```
